```python
import jax, jax.numpy as jnp
from jax import lax
import numpy as np

D_MODEL = 4096
BATCH = 4
SEQ = 4096
DEPTH = 2
DEC_BATCH = 8
DEC_SEQ = 64
PAST_LEN = 2048

CHUNK = 64
D_MIX = D_MODEL
D_POOL = D_MIX // 4
D_SCONV = (D_MIX - D_POOL) // 2
D_CCONV = D_MIX - D_POOL - D_SCONV
POOL_WINDOWS = (2, 4, 8, 16)
N_POOL_GROUPS = len(POOL_WINDOWS)
POOL_GROUP = D_POOL // N_POOL_GROUPS
POOL_HIST = max(POOL_WINDOWS) - 1
SCONV_WIDTH = 3
CCONV_WIDTH = 31
D_IN = D_POOL + 3 * D_SCONV + 2 * D_CCONV
D_FF = -(-8 * D_MODEL // (3 * 256)) * 256
RMS_EPS = 1e-6
LN_EPS = 1e-5

kernel_name = "hymba_pool_conv_conformer_stream_step"


def rmsnorm(x, g):
    xf = x.astype(jnp.float32)
    y = xf * lax.rsqrt(jnp.mean(xf * xf, axis=-1, keepdims=True) + RMS_EPS)
    return (y * g.astype(jnp.float32)).astype(x.dtype)


def layernorm(x, g, b):
    xf = x.astype(jnp.float32)
    mu = jnp.mean(xf, axis=-1, keepdims=True)
    var = jnp.mean(jnp.square(xf - mu), axis=-1, keepdims=True)
    y = (xf - mu) * lax.rsqrt(var + LN_EPS)
    return (y * g.astype(jnp.float32) + b.astype(jnp.float32)).astype(x.dtype)


def causal_depthwise_conv(u, hist, w):
    k = w.shape[0]
    p = jnp.concatenate([hist.astype(u.dtype), u], axis=1)
    y = lax.conv_general_dilated(
        p, w[:, None, :].astype(u.dtype), window_strides=(1,), padding='VALID',
        dimension_numbers=('NWC', 'WIO', 'NWC'), feature_group_count=u.shape[-1])
    return y, p[:, p.shape[1] - (k - 1):]


def pool_mixer(v, hist, pos0, w_grp, scale):
    bsz, t = v.shape[0], v.shape[1]
    p = jnp.concatenate([hist.astype(v.dtype), v], axis=1)
    cs = jnp.cumsum(p.astype(jnp.float32), axis=1)
    cs = jnp.pad(cs, ((0, 0), (1, 0), (0, 0)))
    end = cs[:, POOL_HIST + 1:]
    pos = pos0 + jnp.arange(t)
    means = []
    for g, k in enumerate(POOL_WINDOWS):
        lo, hi = g * POOL_GROUP, (g + 1) * POOL_GROUP
        start = cs[:, POOL_HIST + 1 - k: POOL_HIST + 1 - k + t, lo:hi]
        cnt = jnp.minimum(k, pos + 1).astype(jnp.float32)[None, :, None]
        means.append((end[..., lo:hi] - start) / cnt)
    mean = jnp.stack(means, axis=2)
    d = mean - v.astype(jnp.float32).reshape(bsz, t, N_POOL_GROUPS, POOL_GROUP)
    y = jnp.einsum('btgc,gcd->btgd', d.astype(v.dtype), w_grp)
    return y.reshape(bsz, t, D_POOL) * scale, p[:, p.shape[1] - POOL_HIST:]


def mixer_block(h, hist_pool, hist_sconv, hist_cconv, pos0, w_in, pool_w, pool_scale,
                sconv_w, cconv_w, cconv_b, cnorm_g, cnorm_b, w_out):
    u = jnp.einsum('btd,de->bte', h, w_in)
    i1 = D_POOL
    i2 = i1 + D_SCONV
    i3 = i2 + D_SCONV
    i4 = i3 + D_SCONV
    i5 = i4 + D_CCONV
    v_a = u[..., :i1]
    gate_b, gate_c, x_b = u[..., i1:i2], u[..., i2:i3], u[..., i3:i4]
    c_val, c_gate = u[..., i4:i5], u[..., i5:]
    y_a, new_pool = pool_mixer(v_a, hist_pool, pos0, pool_w, pool_scale)
    z_b, new_sconv = causal_depthwise_conv(gate_c * x_b, hist_sconv, sconv_w)
    y_b = gate_b * z_b
    glu = c_val * jax.nn.sigmoid(c_gate)
    z_c, new_cconv = causal_depthwise_conv(glu, hist_cconv, cconv_w)
    y_c = jax.nn.silu(layernorm(z_c + cconv_b, cnorm_g, cnorm_b))
    y = jnp.concatenate([y_a, y_b, y_c], axis=-1)
    return jnp.einsum('bte,ed->btd', y, w_out), new_pool, new_sconv, new_cconv


def swiglu(h, w_gate, w_up, w_down):
    a = jnp.einsum('btd,df->btf', h, w_gate)
    b = jnp.einsum('btd,df->btf', h, w_up)
    return jnp.einsum('btf,fd->btd', jax.nn.silu(a) * b, w_down)


def trunk(x, hist_pool, hist_sconv, hist_cconv, pos0, norm_mix, norm_ffn, w_in, pool_w,
          pool_scale, sconv_w, cconv_w, cconv_b, cnorm_g, cnorm_b, w_out, w_gate, w_up,
          w_down, norm_final):
    pools, sconvs, cconvs = [], [], []
    for l in range(DEPTH):
        m, sp, ss, sc = mixer_block(rmsnorm(x, norm_mix[l]), hist_pool[l], hist_sconv[l],
                                    hist_cconv[l], pos0, w_in[l], pool_w[l], pool_scale[l],
                                    sconv_w[l], cconv_w[l], cconv_b[l], cnorm_g[l], cnorm_b[l],
                                    w_out[l])
        x = x + m
        x = x + swiglu(rmsnorm(x, norm_ffn[l]), w_gate[l], w_up[l], w_down[l])
        pools.append(sp)
        sconvs.append(ss)
        cconvs.append(sc)
    return rmsnorm(x, norm_final), jnp.stack(pools), jnp.stack(sconvs), jnp.stack(cconvs)


def setup_inputs(seed: int = 0) -> dict:
    key = jax.random.key(seed)
    ks = jax.random.split(key, 24)
    f32 = jnp.float32
    nrm = lambda k, s, sc: jax.random.normal(k, s, f32) * sc
    return {
        "x_prompt": nrm(ks[0], (BATCH, SEQ, D_MODEL), 1.0),
        "x_sample": nrm(ks[1], (DEC_BATCH, DEC_SEQ, D_MODEL), 1.0),
        "cache_pool": nrm(ks[2], (DEPTH, DEC_BATCH, POOL_HIST, D_POOL), 1.0),
        "cache_sconv": nrm(ks[3], (DEPTH, DEC_BATCH, SCONV_WIDTH - 1, D_SCONV), 1.0),
        "cache_cconv": nrm(ks[4], (DEPTH, DEC_BATCH, CCONV_WIDTH - 1, D_CCONV), 1.0),
        "norm_mix": 1.0 + nrm(ks[5], (DEPTH, D_MODEL), 0.1),
        "norm_ffn": 1.0 + nrm(ks[6], (DEPTH, D_MODEL), 0.1),
        "w_in": nrm(ks[7], (DEPTH, D_MODEL, D_IN), D_MODEL ** -0.5),
        "pool_w": nrm(ks[8], (DEPTH, N_POOL_GROUPS, POOL_GROUP, POOL_GROUP), POOL_GROUP ** -0.5),
        "pool_scale": 0.5 + nrm(ks[9], (DEPTH, D_POOL), 0.05),
        "sconv_w": nrm(ks[10], (DEPTH, SCONV_WIDTH, D_SCONV), SCONV_WIDTH ** -0.5),
        "cconv_w": nrm(ks[11], (DEPTH, CCONV_WIDTH, D_CCONV), CCONV_WIDTH ** -0.5),
        "cconv_b": nrm(ks[12], (DEPTH, D_CCONV), 0.01),
        "cnorm_g": 1.0 + nrm(ks[13], (DEPTH, D_CCONV), 0.1),
        "cnorm_b": nrm(ks[14], (DEPTH, D_CCONV), 0.01),
        "w_out": nrm(ks[15], (DEPTH, D_MIX, D_MODEL), D_MIX ** -0.5),
        "w_gate": nrm(ks[16], (DEPTH, D_MODEL, D_FF), D_MODEL ** -0.5),
        "w_up": nrm(ks[17], (DEPTH, D_MODEL, D_FF), D_MODEL ** -0.5),
        "w_down": nrm(ks[18], (DEPTH, D_FF, D_MODEL), D_FF ** -0.5),
        "norm_final": 1.0 + nrm(ks[19], (D_MODEL,), 0.1),
    }


def reference(x_prompt, x_sample, cache_pool, cache_sconv, cache_cconv, norm_mix, norm_ffn,
              w_in, pool_w, pool_scale, sconv_w, cconv_w, cconv_b, cnorm_g, cnorm_b, w_out,
              w_gate, w_up, w_down, norm_final):
    dt = x_prompt.dtype
    zero_pool = jnp.zeros((DEPTH, BATCH, POOL_HIST, D_POOL), dt)
    zero_sconv = jnp.zeros((DEPTH, BATCH, SCONV_WIDTH - 1, D_SCONV), dt)
    zero_cconv = jnp.zeros((DEPTH, BATCH, CCONV_WIDTH - 1, D_CCONV), dt)
    y_prompt, pool_prompt, sconv_prompt, cconv_prompt = trunk(
        x_prompt, zero_pool, zero_sconv, zero_cconv, 0, norm_mix, norm_ffn, w_in, pool_w,
        pool_scale, sconv_w, cconv_w, cconv_b, cnorm_g, cnorm_b, w_out, w_gate, w_up, w_down,
        norm_final)
    y_sample, pool_sample, sconv_sample, cconv_sample = trunk(
        x_sample, cache_pool, cache_sconv, cache_cconv, PAST_LEN, norm_mix, norm_ffn, w_in,
        pool_w, pool_scale, sconv_w, cconv_w, cconv_b, cnorm_g, cnorm_b, w_out, w_gate, w_up,
        w_down, norm_final)
    return (y_prompt, y_sample, pool_prompt, pool_sample, sconv_prompt, sconv_sample,
            cconv_prompt, cconv_sample)
```

```python
import functools

import jax
import jax.numpy as jnp
from jax import lax
from jax.experimental import pallas as pl
from jax.experimental.pallas import tpu as pltpu

POOL_WINDOWS = (2, 4, 8, 16)
POOL_HIST = max(POOL_WINDOWS) - 1
PAST_LEN = 2048
RMS_EPS = 1e-6
LN_EPS = 1e-5

VMEM_LIMIT_BYTES_V7X = 56 * 1024 * 1024
SUBLANES = 8
ROW_CHUNK = 64
CONV_BLOCK = 256
FFN_DOWN_COLS = 1024

F32 = jnp.float32
BF16 = jnp.bfloat16


def _round_up(n, m):
    return -(-n // m) * m


def _rmsnorm_f32(x, g):
    return x * lax.rsqrt(jnp.mean(x * x, axis=-1, keepdims=True) + RMS_EPS) * g


def _dot(a, b):
    return jnp.dot(a, b, preferred_element_type=F32)


def _tile_geometry(n_seq_total, seq_len, tile_rows):
    if tile_rows >= seq_len:
        assert tile_rows % seq_len == 0
        nseq, tc, chunks = tile_rows // seq_len, seq_len, 1
    else:
        assert seq_len % tile_rows == 0
        nseq, tc, chunks = 1, tile_rows, seq_len // tile_rows
    assert n_seq_total % nseq == 0
    return nseq, tc, chunks


def _pool_kernel(x_ref, g_ref, w_ref, pw_ref, ps_ref, cache_ref, ya_ref, newc_ref, p_scr, d_scr,
                 *, nseq, tc, chunks, pos0, group):
    hp = 2 * SUBLANES
    chunk = pl.program_id(0) % chunks
    h = _rmsnorm_f32(x_ref[...], g_ref[...]).astype(BF16)
    v = _dot(h, w_ref[...])
    for s in range(nseq):
        @pl.when(chunk == 0)
        def _():
            p_scr[s, hp - POOL_HIST:hp, :] = cache_ref[s]
        p_scr[s, hp:hp + tc, :] = v[s * tc:(s + 1) * tc, :]
    rc = min(ROW_CHUNK, tc)
    for s in range(nseq):
        for r0 in range(0, tc, rc):
            pos = pos0 + chunk * tc + r0 + lax.broadcasted_iota(jnp.int32, (rc, group), 0)
            for gi, k in enumerate(POOL_WINDOWS):
                lanes = slice(gi * group, (gi + 1) * group)
                cur = p_scr[s, hp + r0:hp + r0 + rc, lanes]
                ws = cur
                for i in range(1, k):
                    ws = ws + p_scr[s, hp + r0 - i:hp + r0 - i + rc, lanes]
                cnt = jnp.minimum(k, pos + 1).astype(F32)
                d_scr[s * tc + r0:s * tc + r0 + rc, lanes] = (ws / cnt - cur).astype(BF16)
    for gi in range(len(POOL_WINDOWS)):
        lanes = slice(gi * group, (gi + 1) * group)
        y = _dot(d_scr[:, lanes], pw_ref[gi]) * ps_ref[:, lanes]
        ya_ref[:, lanes] = y.astype(ya_ref.dtype)
    for s in range(nseq):
        tail = p_scr[s, hp + tc - POOL_HIST:hp + tc, :]
        newc_ref[s] = tail
        p_scr[s, hp - POOL_HIST:hp, :] = tail


def _pool_call(x2d, g, w_in_bf, l, pool_w_bf, pool_scale, cache, *, n_seq, seq_len, tile_rows, pos0):
    rows, d_model = x2d.shape
    d_pool = pool_scale.shape[-1]
    ngroups = len(POOL_WINDOWS)
    group = d_pool // ngroups
    nseq, tc, chunks = _tile_geometry(n_seq, seq_len, tile_rows)
    grid = (rows // tile_rows,)
    kern = functools.partial(_pool_kernel, nseq=nseq, tc=tc, chunks=chunks, pos0=pos0, group=group)
    return pl.pallas_call(
        kern,
        grid=grid,
        in_specs=[
            pl.BlockSpec((tile_rows, d_model), lambda m: (m, 0)),
            pl.BlockSpec((None, 1, d_model), lambda m: (l, 0, 0)),
            pl.BlockSpec((None, d_model, d_pool), lambda m: (l, 0, 0)),
            pl.BlockSpec((None, ngroups, group, group), lambda m: (l, 0, 0, 0)),
            pl.BlockSpec((None, 1, d_pool), lambda m: (l, 0, 0)),
            pl.BlockSpec((None, nseq, POOL_HIST, d_pool), lambda m: (l, m // chunks, 0, 0)),
        ],
        out_specs=[
            pl.BlockSpec((tile_rows, d_pool), lambda m: (m, 0)),
            pl.BlockSpec((nseq, POOL_HIST, d_pool), lambda m: (m // chunks, 0, 0)),
        ],
        out_shape=[
            jax.ShapeDtypeStruct((rows, d_pool), BF16),
            jax.ShapeDtypeStruct((n_seq, POOL_HIST, d_pool), F32),
        ],
        scratch_shapes=[
            pltpu.VMEM((nseq, 2 * SUBLANES + tc, d_pool), F32),
            pltpu.VMEM((tile_rows, d_pool), BF16),
        ],
        compiler_params=pltpu.CompilerParams(
            dimension_semantics=("arbitrary",), vmem_limit_bytes=VMEM_LIMIT_BYTES_V7X),
        name="pool_mixer",
    )(x2d, g, w_in_bf, pool_w_bf, pool_scale, cache)


def _conv_kernel(x_ref, g_ref, wgb_ref, wgc_ref, wxb_ref, wcv_ref, wcg_ref, sw_ref, cw_ref, cb_ref,
                 scache_ref, ccache_ref, yb_ref, zc_ref, news_ref, newc_ref,
                 h_scr, q_scr, e_scr, s_carry, c_carry, *, nseq, tc, chunks, ks, kc):
    hq = _round_up(ks - 1, SUBLANES)
    he = _round_up(kc - 1, SUBLANES)
    chunk = pl.program_id(0) % chunks
    j = pl.program_id(1)

    @pl.when(j == 0)
    def _():
        h_scr[...] = _rmsnorm_f32(x_ref[...], g_ref[...]).astype(BF16)

    h = h_scr[...]
    q = _dot(h, wgc_ref[...]) * _dot(h, wxb_ref[...])
    glu = _dot(h, wcv_ref[...]) * jax.nn.sigmoid(_dot(h, wcg_ref[...]))
    for s in range(nseq):
        if chunks > 1:
            @pl.when(chunk > 0)
            def _():
                q_scr[s, hq - (ks - 1):hq, :] = s_carry[j, s, hq - (ks - 1):hq, :]
                e_scr[s, he - (kc - 1):he, :] = c_carry[j, s, he - (kc - 1):he, :]

        @pl.when(chunk == 0)
        def _():
            q_scr[s, hq - (ks - 1):hq, :] = scache_ref[s]
            e_scr[s, he - (kc - 1):he, :] = ccache_ref[s]
        q_scr[s, hq:hq + tc, :] = q[s * tc:(s + 1) * tc, :]
        e_scr[s, he:he + tc, :] = glu[s * tc:(s + 1) * tc, :]
    gate_b = _dot(h, wgb_ref[...])
    rc = min(ROW_CHUNK, tc)
    for s in range(nseq):
        for r0 in range(0, tc, rc):
            rows = slice(s * tc + r0, s * tc + r0 + rc)
            zb = None
            for k in range(ks):
                t0 = hq - (ks - 1) + k + r0
                term = sw_ref[k:k + 1, :] * q_scr[s, t0:t0 + rc, :]
                zb = term if zb is None else zb + term
            yb_ref[rows, :] = (gate_b[rows, :] * zb).astype(yb_ref.dtype)
            zc = None
            for k in range(kc):
                t0 = he - (kc - 1) + k + r0
                term = cw_ref[k:k + 1, :] * e_scr[s, t0:t0 + rc, :]
                zc = term if zc is None else zc + term
            zc_ref[rows, :] = zc + cb_ref[...]
    for s in range(nseq):
        s_tail = q_scr[s, hq + tc - (ks - 1):hq + tc, :]
        c_tail = e_scr[s, he + tc - (kc - 1):he + tc, :]
        news_ref[s] = s_tail
        newc_ref[s] = c_tail
        if chunks > 1:
            s_carry[j, s, hq - (ks - 1):hq, :] = s_tail
            c_carry[j, s, he - (kc - 1):he, :] = c_tail


def _conv_call(x2d, g, w_in_bf, l, sconv_w, cconv_w, cconv_b, scache, ccache, *, d_pool, n_seq, seq_len, tile_rows):
    rows, d_model = x2d.shape
    ks, d_conv = sconv_w.shape[-2:]
    kc = cconv_w.shape[-2]
    cb = CONV_BLOCK
    nj = d_conv // cb
    nseq, tc, chunks = _tile_geometry(n_seq, seq_len, tile_rows)
    grid = (rows // tile_rows, nj)
    hq = _round_up(ks - 1, SUBLANES)
    he = _round_up(kc - 1, SUBLANES)

    def w_spec(slot):
        off = (d_pool + slot * d_conv) // cb
        return pl.BlockSpec((None, d_model, cb), lambda m, j: (l, 0, off + j))

    kern = functools.partial(_conv_kernel, nseq=nseq, tc=tc, chunks=chunks, ks=ks, kc=kc)
    return pl.pallas_call(
        kern,
        grid=grid,
        in_specs=[
            pl.BlockSpec((tile_rows, d_model), lambda m, j: (m, 0)),
            pl.BlockSpec((None, 1, d_model), lambda m, j: (l, 0, 0)),
            w_spec(0), w_spec(1), w_spec(2), w_spec(3), w_spec(4),
            pl.BlockSpec((None, ks, cb), lambda m, j: (l, 0, j)),
            pl.BlockSpec((None, kc, cb), lambda m, j: (l, 0, j)),
            pl.BlockSpec((None, 1, cb), lambda m, j: (l, 0, j)),
            pl.BlockSpec((None, nseq, ks - 1, cb), lambda m, j: (l, m // chunks, 0, j)),
            pl.BlockSpec((None, nseq, kc - 1, cb), lambda m, j: (l, m // chunks, 0, j)),
        ],
        out_specs=[
            pl.BlockSpec((tile_rows, cb), lambda m, j: (m, j)),
            pl.BlockSpec((tile_rows, cb), lambda m, j: (m, j)),
            pl.BlockSpec((nseq, ks - 1, cb), lambda m, j: (m // chunks, 0, j)),
            pl.BlockSpec((nseq, kc - 1, cb), lambda m, j: (m // chunks, 0, j)),
        ],
        out_shape=[
            jax.ShapeDtypeStruct((rows, d_conv), BF16),
            jax.ShapeDtypeStruct((rows, d_conv), F32),
            jax.ShapeDtypeStruct((n_seq, ks - 1, d_conv), F32),
            jax.ShapeDtypeStruct((n_seq, kc - 1, d_conv), F32),
        ],
        scratch_shapes=[
            pltpu.VMEM((tile_rows, d_model), BF16),
            pltpu.VMEM((nseq, hq + tc, cb), F32),
            pltpu.VMEM((nseq, he + tc, cb), F32),
            pltpu.VMEM((nj, nseq, hq, cb), F32),
            pltpu.VMEM((nj, nseq, he, cb), F32),
        ],
        compiler_params=pltpu.CompilerParams(
            dimension_semantics=("arbitrary", "arbitrary"), vmem_limit_bytes=VMEM_LIMIT_BYTES_V7X),
        name="conv_mixers",
    )(x2d, g, w_in_bf, w_in_bf, w_in_bf, w_in_bf, w_in_bf, sconv_w, cconv_w, cconv_b, scache, ccache)


def _out_kernel(x_ref, ya_ref, yb_ref, zc_ref, lg_ref, lb_ref, w_ref, o_ref, y_scr, *, d_pool, d_conv):
    @pl.when(pl.program_id(1) == 0)
    def _():
        y_scr[:, :d_pool] = ya_ref[...]
        y_scr[:, d_pool:d_pool + d_conv] = yb_ref[...]
        z = zc_ref[...]
        mu = jnp.mean(z, axis=-1, keepdims=True)
        zc = z - mu
        var = jnp.mean(zc * zc, axis=-1, keepdims=True)
        ln = zc * lax.rsqrt(var + LN_EPS) * lg_ref[...] + lb_ref[...]
        y_scr[:, d_pool + d_conv:] = (ln * jax.nn.sigmoid(ln)).astype(BF16)

    o_ref[...] = x_ref[...] + _dot(y_scr[...], w_ref[...])


def _out_call(x2d, ya, yb, zc, cnorm_g, cnorm_b, w_out_bf, l, *, tile_rows, tile_cols):
    rows, d_model = x2d.shape
    d_pool, d_conv = ya.shape[-1], yb.shape[-1]
    d_mix = w_out_bf.shape[-2]
    grid = (rows // tile_rows, d_model // tile_cols)
    kern = functools.partial(_out_kernel, d_pool=d_pool, d_conv=d_conv)
    return pl.pallas_call(
        kern,
        grid=grid,
        in_specs=[
            pl.BlockSpec((tile_rows, tile_cols), lambda m, n: (m, n)),
            pl.BlockSpec((tile_rows, d_pool), lambda m, n: (m, 0)),
            pl.BlockSpec((tile_rows, d_conv), lambda m, n: (m, 0)),
            pl.BlockSpec((tile_rows, d_conv), lambda m, n: (m, 0)),
            pl.BlockSpec((None, 1, d_conv), lambda m, n: (l, 0, 0)),
            pl.BlockSpec((None, 1, d_conv), lambda m, n: (l, 0, 0)),
            pl.BlockSpec((None, d_mix, tile_cols), lambda m, n: (l, 0, n)),
        ],
        out_specs=pl.BlockSpec((tile_rows, tile_cols), lambda m, n: (m, n)),
        out_shape=jax.ShapeDtypeStruct((rows, d_model), F32),
        scratch_shapes=[pltpu.VMEM((tile_rows, d_mix), BF16)],
        compiler_params=pltpu.CompilerParams(
            dimension_semantics=("arbitrary", "arbitrary"), vmem_limit_bytes=VMEM_LIMIT_BYTES_V7X),
        name="out_proj",
    )(x2d, ya, yb, zc, cnorm_g, cnorm_b, w_out_bf)


def _ffn_kernel(x_ref, g_ref, wg_ref, wu_ref, wd_ref, gf_ref, o_ref, h_scr, *, final_norm):
    f = pl.program_id(1)

    @pl.when(f == 0)
    def _():
        h_scr[...] = _rmsnorm_f32(x_ref[...], g_ref[...]).astype(BF16)

    h = h_scr[...]
    a = _dot(h, wg_ref[...])
    b = _dot(h, wu_ref[...])
    act = (a * jax.nn.sigmoid(a) * b).astype(BF16)
    for c0 in range(0, o_ref.shape[-1], FFN_DOWN_COLS):
        cols = slice(c0, c0 + FFN_DOWN_COLS)
        part = _dot(act, wd_ref[:, cols])

        @pl.when(f == 0)
        def _():
            o_ref[:, cols] = part

        @pl.when(f > 0)
        def _():
            o_ref[:, cols] += part

    @pl.when(f == pl.num_programs(1) - 1)
    def _():
        y = x_ref[...] + o_ref[...]
        if final_norm:
            y = _rmsnorm_f32(y, gf_ref[...])
        o_ref[...] = y


def _ffn_call(x2d, g, w_gate_bf, w_up_bf, w_down_bf, l, norm_final, *, tile_rows, tile_ff, final_norm):
    rows, d_model = x2d.shape
    d_ff = w_gate_bf.shape[-1]
    assert d_ff % tile_ff == 0
    grid = (rows // tile_rows, d_ff // tile_ff)
    kern = functools.partial(_ffn_kernel, final_norm=final_norm)
    return pl.pallas_call(
        kern,
        grid=grid,
        in_specs=[
            pl.BlockSpec((tile_rows, d_model), lambda m, f: (m, 0), pipeline_mode=pl.Buffered(1)),
            pl.BlockSpec((None, 1, d_model), lambda m, f: (l, 0, 0)),
            pl.BlockSpec((None, d_model, tile_ff), lambda m, f: (l, 0, f)),
            pl.BlockSpec((None, d_model, tile_ff), lambda m, f: (l, 0, f)),
            pl.BlockSpec((None, tile_ff, d_model), lambda m, f: (l, f, 0)),
            pl.BlockSpec((1, d_model), lambda m, f: (0, 0)),
        ],
        out_specs=pl.BlockSpec((tile_rows, d_model), lambda m, f: (m, 0)),
        out_shape=jax.ShapeDtypeStruct((rows, d_model), F32),
        scratch_shapes=[pltpu.VMEM((tile_rows, d_model), BF16)],
        compiler_params=pltpu.CompilerParams(
            dimension_semantics=("arbitrary", "arbitrary"), vmem_limit_bytes=VMEM_LIMIT_BYTES_V7X),
        name="swiglu",
    )(x2d, g, w_gate_bf, w_up_bf, w_down_bf, norm_final)


def _trunk(x, cache_pool, cache_sconv, cache_cconv, pos0, p, *, tile_rows):
    n_seq, seq_len, d_model = x.shape
    depth = p["w_in"].shape[0]
    d_pool = p["pool_scale"].shape[-1]
    x2d = x.reshape(n_seq * seq_len, d_model)
    geo = dict(n_seq=n_seq, seq_len=seq_len, tile_rows=tile_rows)
    pools, sconvs, cconvs = [], [], []
    for l in range(depth):
        ya, new_pool = _pool_call(x2d, p["norm_mix"], p["w_in"], l, p["pool_w"], p["pool_scale"], cache_pool,
                                  pos0=pos0, **geo)
        yb, zc, new_sconv, new_cconv = _conv_call(x2d, p["norm_mix"], p["w_in"], l, p["sconv_w"], p["cconv_w"],
                                                  p["cconv_b"], cache_sconv, cache_cconv, d_pool=d_pool, **geo)
        x2d = _out_call(x2d, ya, yb, zc, p["cnorm_g"], p["cnorm_b"], p["w_out"], l,
                        tile_rows=tile_rows, tile_cols=1024)
        x2d = _ffn_call(x2d, p["norm_ffn"], p["w_gate"], p["w_up"], p["w_down"], l, p["norm_final"],
                        tile_rows=tile_rows, tile_ff=256, final_norm=(l == depth - 1))
        pools.append(new_pool)
        sconvs.append(new_sconv)
        cconvs.append(new_cconv)
    return x2d.reshape(x.shape), jnp.stack(pools), jnp.stack(sconvs), jnp.stack(cconvs)


def kernel(x_prompt, x_sample, cache_pool, cache_sconv, cache_cconv, norm_mix, norm_ffn, w_in, pool_w, pool_scale,
           sconv_w, cconv_w, cconv_b, cnorm_g, cnorm_b, w_out, w_gate, w_up, w_down, norm_final):
    depth, d_model = norm_mix.shape
    batch = x_prompt.shape[0]
    dt = x_prompt.dtype
    row = lambda a: a.reshape(a.shape[0], 1, a.shape[-1])
    p = dict(
        norm_mix=row(norm_mix), norm_ffn=row(norm_ffn), w_in=w_in.astype(BF16), pool_w=pool_w.astype(BF16),
        pool_scale=row(pool_scale), sconv_w=sconv_w, cconv_w=cconv_w, cconv_b=row(cconv_b), cnorm_g=row(cnorm_g),
        cnorm_b=row(cnorm_b), w_out=w_out.astype(BF16), w_gate=w_gate.astype(BF16), w_up=w_up.astype(BF16),
        w_down=w_down.astype(BF16), norm_final=norm_final.reshape(1, d_model))
    zero_pool = jnp.zeros((depth, batch) + cache_pool.shape[2:], dt)
    zero_sconv = jnp.zeros((depth, batch) + cache_sconv.shape[2:], dt)
    zero_cconv = jnp.zeros((depth, batch) + cache_cconv.shape[2:], dt)
    y_p, pool_p, sconv_p, cconv_p = _trunk(x_prompt, zero_pool, zero_sconv, zero_cconv, 0, p, tile_rows=512)
    y_s, pool_s, sconv_s, cconv_s = _trunk(x_sample, cache_pool, cache_sconv, cache_cconv, PAST_LEN, p,
                                           tile_rows=512)
    return (y_p, y_s, pool_p, pool_s, sconv_p, sconv_s, cconv_p, cconv_s)
```

```python
import functools

import jax
import jax.numpy as jnp
from jax import lax
from jax.experimental import pallas as pl
from jax.experimental.pallas import tpu as pltpu

POOL_WINDOWS = (2, 4, 8, 16)
POOL_HIST = max(POOL_WINDOWS) - 1
PAST_LEN = 2048
RMS_EPS = 1e-6
LN_EPS = 1e-5

VMEM_LIMIT_BYTES_V7X = 56 * 1024 * 1024
SUBLANES = 8
ROW_CHUNK = 64
CONV_BLOCK = 256
FFN_DOWN_COLS = 1024

F32 = jnp.float32
BF16 = jnp.bfloat16


def _round_up(n, m):
    return -(-n // m) * m


def _rmsnorm_f32(x, g):
    return x * lax.rsqrt(jnp.mean(x * x, axis=-1, keepdims=True) + RMS_EPS) * g


def _dot(a, b):
    return jnp.dot(a, b, preferred_element_type=F32)


def _tile_geometry(n_seq_total, seq_len, tile_rows):
    if tile_rows >= seq_len:
        assert tile_rows % seq_len == 0
        nseq, tc, chunks = tile_rows // seq_len, seq_len, 1
    else:
        assert seq_len % tile_rows == 0
        nseq, tc, chunks = 1, tile_rows, seq_len // tile_rows
    assert n_seq_total % nseq == 0
    return nseq, tc, chunks


def _pool_kernel(x_ref, g_ref, w_ref, pw_ref, ps_ref, cache_ref, ya_ref, newc_ref, p_scr, d_scr,
                 *, nseq, tc, chunks, pos0, group):
    hp = 2 * SUBLANES
    chunk = pl.program_id(0) % chunks
    h = _rmsnorm_f32(x_ref[...], g_ref[...]).astype(BF16)
    v = _dot(h, w_ref[...])
    for s in range(nseq):
        @pl.when(chunk == 0)
        def _():
            p_scr[s, hp - POOL_HIST:hp, :] = cache_ref[s]
        p_scr[s, hp:hp + tc, :] = v[s * tc:(s + 1) * tc, :]
    rc = min(ROW_CHUNK, tc)
    for s in range(nseq):
        for r0 in range(0, tc, rc):
            pos = pos0 + chunk * tc + r0 + lax.broadcasted_iota(jnp.int32, (rc, group), 0)
            for gi, k in enumerate(POOL_WINDOWS):
                lanes = slice(gi * group, (gi + 1) * group)
                cur = p_scr[s, hp + r0:hp + r0 + rc, lanes]
                ws = cur
                for i in range(1, k):
                    ws = ws + p_scr[s, hp + r0 - i:hp + r0 - i + rc, lanes]
                cnt = jnp.minimum(k, pos + 1).astype(F32)
                d_scr[s * tc + r0:s * tc + r0 + rc, lanes] = (ws / cnt - cur).astype(BF16)
    for gi in range(len(POOL_WINDOWS)):
        lanes = slice(gi * group, (gi + 1) * group)
        y = _dot(d_scr[:, lanes], pw_ref[gi]) * ps_ref[:, lanes]
        ya_ref[:, lanes] = y.astype(ya_ref.dtype)
    for s in range(nseq):
        tail = p_scr[s, hp + tc - POOL_HIST:hp + tc, :]
        newc_ref[s] = tail
        p_scr[s, hp - POOL_HIST:hp, :] = tail


def _pool_call(x2d, g, w_in_bf, l, pool_w_bf, pool_scale, cache, *, n_seq, seq_len, tile_rows, pos0):
    rows, d_model = x2d.shape
    d_pool = pool_scale.shape[-1]
    ngroups = len(POOL_WINDOWS)
    group = d_pool // ngroups
    nseq, tc, chunks = _tile_geometry(n_seq, seq_len, tile_rows)
    grid = (rows // tile_rows,)
    kern = functools.partial(_pool_kernel, nseq=nseq, tc=tc, chunks=chunks, pos0=pos0, group=group)
    return pl.pallas_call(
        kern,
        grid=grid,
        in_specs=[
            pl.BlockSpec((tile_rows, d_model), lambda m: (m, 0)),
            pl.BlockSpec((None, 1, d_model), lambda m: (l, 0, 0)),
            pl.BlockSpec((None, d_model, d_pool), lambda m: (l, 0, 0)),
            pl.BlockSpec((None, ngroups, group, group), lambda m: (l, 0, 0, 0)),
            pl.BlockSpec((None, 1, d_pool), lambda m: (l, 0, 0)),
            pl.BlockSpec((None, nseq, POOL_HIST, d_pool), lambda m: (l, m // chunks, 0, 0)),
        ],
        out_specs=[
            pl.BlockSpec((tile_rows, d_pool), lambda m: (m, 0)),
            pl.BlockSpec((nseq, POOL_HIST, d_pool), lambda m: (m // chunks, 0, 0)),
        ],
        out_shape=[
            jax.ShapeDtypeStruct((rows, d_pool), BF16),
            jax.ShapeDtypeStruct((n_seq, POOL_HIST, d_pool), F32),
        ],
        scratch_shapes=[
            pltpu.VMEM((nseq, 2 * SUBLANES + tc, d_pool), F32),
            pltpu.VMEM((tile_rows, d_pool), BF16),
        ],
        compiler_params=pltpu.CompilerParams(
            dimension_semantics=("arbitrary",), vmem_limit_bytes=VMEM_LIMIT_BYTES_V7X),
        name="pool_mixer",
    )(x2d, g, w_in_bf, pool_w_bf, pool_scale, cache)


def _conv_kernel(x_ref, g_ref, wgb_ref, wgc_ref, wxb_ref, wcv_ref, wcg_ref, sw_ref, cw_ref, cb_ref,
                 scache_ref, ccache_ref, yb_ref, zc_ref, news_ref, newc_ref,
                 h_scr, q_scr, e_scr, s_carry, c_carry, *, nseq, tc, chunks, ks, kc):
    hq = _round_up(ks - 1, SUBLANES)
    he = _round_up(kc - 1, SUBLANES)
    chunk = pl.program_id(0) % chunks
    j = pl.program_id(1)

    @pl.when(j == 0)
    def _():
        h_scr[...] = _rmsnorm_f32(x_ref[...], g_ref[...]).astype(BF16)

    h = h_scr[...]
    q = _dot(h, wgc_ref[...]) * _dot(h, wxb_ref[...])
    glu = _dot(h, wcv_ref[...]) * jax.nn.sigmoid(_dot(h, wcg_ref[...]))
    for s in range(nseq):
        if chunks > 1:
            @pl.when(chunk > 0)
            def _():
                q_scr[s, hq - (ks - 1):hq, :] = s_carry[j, s, hq - (ks - 1):hq, :]
                e_scr[s, he - (kc - 1):he, :] = c_carry[j, s, he - (kc - 1):he, :]

        @pl.when(chunk == 0)
        def _():
            q_scr[s, hq - (ks - 1):hq, :] = scache_ref[s]
            e_scr[s, he - (kc - 1):he, :] = ccache_ref[s]
        q_scr[s, hq:hq + tc, :] = q[s * tc:(s + 1) * tc, :]
        e_scr[s, he:he + tc, :] = glu[s * tc:(s + 1) * tc, :]
    gate_b = _dot(h, wgb_ref[...])
    rc = min(ROW_CHUNK, tc)
    for s in range(nseq):
        for r0 in range(0, tc, rc):
            rows = slice(s * tc + r0, s * tc + r0 + rc)
            zb = None
            for k in range(ks):
                t0 = hq - (ks - 1) + k + r0
                term = sw_ref[k:k + 1, :] * q_scr[s, t0:t0 + rc, :]
                zb = term if zb is None else zb + term
            yb_ref[rows, :] = (gate_b[rows, :] * zb).astype(yb_ref.dtype)
            zc = None
            for k in range(kc):
                t0 = he - (kc - 1) + k + r0
                term = cw_ref[k:k + 1, :] * e_scr[s, t0:t0 + rc, :]
                zc = term if zc is None else zc + term
            zc_ref[rows, :] = zc + cb_ref[...]
    for s in range(nseq):
        s_tail = q_scr[s, hq + tc - (ks - 1):hq + tc, :]
        c_tail = e_scr[s, he + tc - (kc - 1):he + tc, :]
        news_ref[s] = s_tail
        newc_ref[s] = c_tail
        if chunks > 1:
            s_carry[j, s, hq - (ks - 1):hq, :] = s_tail
            c_carry[j, s, he - (kc - 1):he, :] = c_tail


def _conv_call(x2d, g, w_in_bf, l, sconv_w, cconv_w, cconv_b, scache, ccache, *, d_pool, n_seq, seq_len, tile_rows):
    rows, d_model = x2d.shape
    ks, d_conv = sconv_w.shape[-2:]
    kc = cconv_w.shape[-2]
    cb = CONV_BLOCK
    nj = d_conv // cb
    nseq, tc, chunks = _tile_geometry(n_seq, seq_len, tile_rows)
    grid = (rows // tile_rows, nj)
    hq = _round_up(ks - 1, SUBLANES)
    he = _round_up(kc - 1, SUBLANES)

    def w_spec(slot):
        off = (d_pool + slot * d_conv) // cb
        return pl.BlockSpec((None, d_model, cb), lambda m, j: (l, 0, off + j))

    kern = functools.partial(_conv_kernel, nseq=nseq, tc=tc, chunks=chunks, ks=ks, kc=kc)
    yb, zc, s_tails, c_tails = pl.pallas_call(
        kern,
        grid=grid,
        in_specs=[
            pl.BlockSpec((tile_rows, d_model), lambda m, j: (m, 0)),
            pl.BlockSpec((None, 1, d_model), lambda m, j: (l, 0, 0)),
            w_spec(0), w_spec(1), w_spec(2), w_spec(3), w_spec(4),
            pl.BlockSpec((None, ks, cb), lambda m, j: (l, 0, j)),
            pl.BlockSpec((None, kc, cb), lambda m, j: (l, 0, j)),
            pl.BlockSpec((None, 1, cb), lambda m, j: (l, 0, j)),
            pl.BlockSpec((None, nseq, ks - 1, cb), lambda m, j: (l, m // chunks, 0, j)),
            pl.BlockSpec((None, nseq, kc - 1, cb), lambda m, j: (l, m // chunks, 0, j)),
        ],
        out_specs=[
            pl.BlockSpec((tile_rows, cb), lambda m, j: (m, j)),
            pl.BlockSpec((tile_rows, cb), lambda m, j: (m, j)),
            pl.BlockSpec((None, nseq, ks - 1, cb), lambda m, j: (m, 0, 0, j)),
            pl.BlockSpec((None, nseq, kc - 1, cb), lambda m, j: (m, 0, 0, j)),
        ],
        out_shape=[
            jax.ShapeDtypeStruct((rows, d_conv), BF16),
            jax.ShapeDtypeStruct((rows, d_conv), F32),
            jax.ShapeDtypeStruct((grid[0], nseq, ks - 1, d_conv), F32),
            jax.ShapeDtypeStruct((grid[0], nseq, kc - 1, d_conv), F32),
        ],
        scratch_shapes=[
            pltpu.VMEM((tile_rows, d_model), BF16),
            pltpu.VMEM((nseq, hq + tc, cb), F32),
            pltpu.VMEM((nseq, he + tc, cb), F32),
            pltpu.VMEM((nj, nseq, hq, cb), F32),
            pltpu.VMEM((nj, nseq, he, cb), F32),
        ],
        compiler_params=pltpu.CompilerParams(
            dimension_semantics=("arbitrary", "arbitrary"), vmem_limit_bytes=VMEM_LIMIT_BYTES_V7X),
        name="conv_mixers",
    )(x2d, g, w_in_bf, w_in_bf, w_in_bf, w_in_bf, w_in_bf, sconv_w, cconv_w, cconv_b, scache, ccache)
    new_sconv = s_tails[chunks - 1::chunks].reshape(n_seq, ks - 1, d_conv)
    new_cconv = c_tails[chunks - 1::chunks].reshape(n_seq, kc - 1, d_conv)
    return yb, zc, new_sconv, new_cconv


def _out_kernel(x_ref, ya_ref, yb_ref, zc_ref, lg_ref, lb_ref, w_ref, o_ref, y_scr, *, d_pool, d_conv):
    @pl.when(pl.program_id(1) == 0)
    def _():
        y_scr[:, :d_pool] = ya_ref[...]
        y_scr[:, d_pool:d_pool + d_conv] = yb_ref[...]
        z = zc_ref[...]
        mu = jnp.mean(z, axis=-1, keepdims=True)
        zc = z - mu
        var = jnp.mean(zc * zc, axis=-1, keepdims=True)
        ln = zc * lax.rsqrt(var + LN_EPS) * lg_ref[...] + lb_ref[...]
        y_scr[:, d_pool + d_conv:] = (ln * jax.nn.sigmoid(ln)).astype(BF16)

    o_ref[...] = x_ref[...] + _dot(y_scr[...], w_ref[...])


def _out_call(x2d, ya, yb, zc, cnorm_g, cnorm_b, w_out_bf, l, *, tile_rows, tile_cols):
    rows, d_model = x2d.shape
    d_pool, d_conv = ya.shape[-1], yb.shape[-1]
    d_mix = w_out_bf.shape[-2]
    grid = (rows // tile_rows, d_model // tile_cols)
    kern = functools.partial(_out_kernel, d_pool=d_pool, d_conv=d_conv)
    return pl.pallas_call(
        kern,
        grid=grid,
        in_specs=[
            pl.BlockSpec((tile_rows, tile_cols), lambda m, n: (m, n)),
            pl.BlockSpec((tile_rows, d_pool), lambda m, n: (m, 0)),
            pl.BlockSpec((tile_rows, d_conv), lambda m, n: (m, 0)),
            pl.BlockSpec((tile_rows, d_conv), lambda m, n: (m, 0)),
            pl.BlockSpec((None, 1, d_conv), lambda m, n: (l, 0, 0)),
            pl.BlockSpec((None, 1, d_conv), lambda m, n: (l, 0, 0)),
            pl.BlockSpec((None, d_mix, tile_cols), lambda m, n: (l, 0, n)),
        ],
        out_specs=pl.BlockSpec((tile_rows, tile_cols), lambda m, n: (m, n)),
        out_shape=jax.ShapeDtypeStruct((rows, d_model), F32),
        scratch_shapes=[pltpu.VMEM((tile_rows, d_mix), BF16)],
        compiler_params=pltpu.CompilerParams(
            dimension_semantics=("arbitrary", "arbitrary"), vmem_limit_bytes=VMEM_LIMIT_BYTES_V7X),
        name="out_proj",
    )(x2d, ya, yb, zc, cnorm_g, cnorm_b, w_out_bf)


def _ffn_kernel(x_ref, g_ref, wg_ref, wu_ref, wd_ref, gf_ref, o_ref, h_scr, *, final_norm):
    f = pl.program_id(1)

    @pl.when(f == 0)
    def _():
        h_scr[...] = _rmsnorm_f32(x_ref[...], g_ref[...]).astype(BF16)
        o_ref[...] = jnp.zeros_like(o_ref)

    h = h_scr[...]
    a = _dot(h, wg_ref[...])
    b = _dot(h, wu_ref[...])
    act = (a * jax.nn.sigmoid(a) * b).astype(BF16)
    for c0 in range(0, o_ref.shape[-1], FFN_DOWN_COLS):
        cols = slice(c0, c0 + FFN_DOWN_COLS)
        o_ref[:, cols] += _dot(act, wd_ref[:, cols])

    @pl.when(f == pl.num_programs(1) - 1)
    def _():
        y = x_ref[...] + o_ref[...]
        if final_norm:
            y = _rmsnorm_f32(y, gf_ref[...])
        o_ref[...] = y


def _ffn_call(x2d, g, w_gate_bf, w_up_bf, w_down_bf, l, norm_final, *, tile_rows, tile_ff, final_norm):
    rows, d_model = x2d.shape
    d_ff = w_gate_bf.shape[-1]
    assert d_ff % tile_ff == 0
    grid = (rows // tile_rows, d_ff // tile_ff)
    kern = functools.partial(_ffn_kernel, final_norm=final_norm)
    return pl.pallas_call(
        kern,
        grid=grid,
        in_specs=[
            pl.BlockSpec((tile_rows, d_model), lambda m, f: (m, 0), pipeline_mode=pl.Buffered(1)),
            pl.BlockSpec((None, 1, d_model), lambda m, f: (l, 0, 0)),
            pl.BlockSpec((None, d_model, tile_ff), lambda m, f: (l, 0, f)),
            pl.BlockSpec((None, d_model, tile_ff), lambda m, f: (l, 0, f)),
            pl.BlockSpec((None, tile_ff, d_model), lambda m, f: (l, f, 0)),
            pl.BlockSpec((1, d_model), lambda m, f: (0, 0)),
        ],
        out_specs=pl.BlockSpec((tile_rows, d_model), lambda m, f: (m, 0)),
        out_shape=jax.ShapeDtypeStruct((rows, d_model), F32),
        scratch_shapes=[pltpu.VMEM((tile_rows, d_model), BF16)],
        compiler_params=pltpu.CompilerParams(
            dimension_semantics=("arbitrary", "arbitrary"), vmem_limit_bytes=VMEM_LIMIT_BYTES_V7X),
        name="swiglu",
    )(x2d, g, w_gate_bf, w_up_bf, w_down_bf, norm_final)


def _trunk(x, cache_pool, cache_sconv, cache_cconv, pos0, p, *, tile_rows):
    n_seq, seq_len, d_model = x.shape
    depth = p["w_in"].shape[0]
    d_pool = p["pool_scale"].shape[-1]
    x2d = x.reshape(n_seq * seq_len, d_model)
    geo = dict(n_seq=n_seq, seq_len=seq_len, tile_rows=tile_rows)
    pools, sconvs, cconvs = [], [], []
    for l in range(depth):
        ya, new_pool = _pool_call(x2d, p["norm_mix"], p["w_in"], l, p["pool_w"], p["pool_scale"], cache_pool,
                                  pos0=pos0, **geo)
        yb, zc, new_sconv, new_cconv = _conv_call(x2d, p["norm_mix"], p["w_in"], l, p["sconv_w"], p["cconv_w"],
                                                  p["cconv_b"], cache_sconv, cache_cconv, d_pool=d_pool, **geo)
        x2d = _out_call(x2d, ya, yb, zc, p["cnorm_g"], p["cnorm_b"], p["w_out"], l,
                        tile_rows=tile_rows, tile_cols=1024)
        x2d = _ffn_call(x2d, p["norm_ffn"], p["w_gate"], p["w_up"], p["w_down"], l, p["norm_final"],
                        tile_rows=tile_rows, tile_ff=256, final_norm=(l == depth - 1))
        pools.append(new_pool)
        sconvs.append(new_sconv)
        cconvs.append(new_cconv)
    return x2d.reshape(x.shape), jnp.stack(pools), jnp.stack(sconvs), jnp.stack(cconvs)


def kernel(x_prompt, x_sample, cache_pool, cache_sconv, cache_cconv, norm_mix, norm_ffn, w_in, pool_w, pool_scale,
           sconv_w, cconv_w, cconv_b, cnorm_g, cnorm_b, w_out, w_gate, w_up, w_down, norm_final):
    depth, d_model = norm_mix.shape
    batch = x_prompt.shape[0]
    dt = x_prompt.dtype
    row = lambda a: a.reshape(a.shape[0], 1, a.shape[-1])
    p = dict(
        norm_mix=row(norm_mix), norm_ffn=row(norm_ffn), w_in=w_in.astype(BF16), pool_w=pool_w.astype(BF16),
        pool_scale=row(pool_scale), sconv_w=sconv_w, cconv_w=cconv_w, cconv_b=row(cconv_b), cnorm_g=row(cnorm_g),
        cnorm_b=row(cnorm_b), w_out=w_out.astype(BF16), w_gate=w_gate.astype(BF16), w_up=w_up.astype(BF16),
        w_down=w_down.astype(BF16), norm_final=norm_final.reshape(1, d_model))
    zero_pool = jnp.zeros((depth, batch) + cache_pool.shape[2:], dt)
    zero_sconv = jnp.zeros((depth, batch) + cache_sconv.shape[2:], dt)
    zero_cconv = jnp.zeros((depth, batch) + cache_cconv.shape[2:], dt)
    y_p, pool_p, sconv_p, cconv_p = _trunk(x_prompt, zero_pool, zero_sconv, zero_cconv, 0, p, tile_rows=512)
    y_s, pool_s, sconv_s, cconv_s = _trunk(x_sample, cache_pool, cache_sconv, cache_cconv, PAST_LEN, p,
                                           tile_rows=512)
    return (y_p, y_s, pool_p, pool_s, sconv_p, sconv_s, cconv_p, cconv_s)
```

```python
import functools

import jax
import jax.numpy as jnp
from jax import lax
from jax.experimental import pallas as pl
from jax.experimental.pallas import tpu as pltpu

POOL_WINDOWS = (2, 4, 8, 16)
POOL_HIST = max(POOL_WINDOWS) - 1
PAST_LEN = 2048
RMS_EPS = 1e-6
LN_EPS = 1e-5

VMEM_LIMIT_BYTES_V7X = 56 * 1024 * 1024
SUBLANES = 8
NORM_ROWS = 16
NORM_UNROLL = 8
ROW_CHUNK = 64
CONV_BLOCK = 256
FFN_DOWN_COLS = 1024

F32 = jnp.float32
BF16 = jnp.bfloat16


def _round_up(n, m):
    return -(-n // m) * m


def _rmsnorm_f32(x, g):
    return x * lax.rsqrt(jnp.mean(x * x, axis=-1, keepdims=True) + RMS_EPS) * g


def _for_row_chunks(n_rows, body):
    def step(i, carry):
        body(pl.ds(pl.multiple_of(i * NORM_ROWS, NORM_ROWS), NORM_ROWS))
        return carry
    lax.fori_loop(0, n_rows // NORM_ROWS, step, 0, unroll=NORM_UNROLL)


def _rmsnorm_rows(x_ref, g_ref, dst_ref):
    def body(rows):
        dst_ref[rows, :] = _rmsnorm_f32(x_ref[rows, :], g_ref[...]).astype(dst_ref.dtype)
    _for_row_chunks(x_ref.shape[0], body)


def _dot(a, b):
    return jnp.dot(a, b, preferred_element_type=F32)


def _tile_geometry(n_seq_total, seq_len, tile_rows):
    if tile_rows >= seq_len:
        assert tile_rows % seq_len == 0
        nseq, tc, chunks = tile_rows // seq_len, seq_len, 1
    else:
        assert seq_len % tile_rows == 0
        nseq, tc, chunks = 1, tile_rows, seq_len // tile_rows
    assert n_seq_total % nseq == 0
    return nseq, tc, chunks


def _pool_kernel(x_ref, g_ref, w_ref, pw_ref, ps_ref, cache_ref, ya_ref, newc_ref, h_scr, p_scr, d_scr,
                 *, nseq, tc, chunks, pos0, group):
    hp = 2 * SUBLANES
    chunk = pl.program_id(0) % chunks
    _rmsnorm_rows(x_ref, g_ref, h_scr)
    v = _dot(h_scr[...], w_ref[...])
    for s in range(nseq):
        @pl.when(chunk == 0)
        def _():
            p_scr[s, hp - POOL_HIST:hp, :] = cache_ref[s]
        p_scr[s, hp:hp + tc, :] = v[s * tc:(s + 1) * tc, :]
    rc = min(ROW_CHUNK, tc)
    for s in range(nseq):
        for r0 in range(0, tc, rc):
            pos = pos0 + chunk * tc + r0 + lax.broadcasted_iota(jnp.int32, (rc, group), 0)
            for gi, k in enumerate(POOL_WINDOWS):
                lanes = slice(gi * group, (gi + 1) * group)
                cur = p_scr[s, hp + r0:hp + r0 + rc, lanes]
                ws = cur
                for i in range(1, k):
                    ws = ws + p_scr[s, hp + r0 - i:hp + r0 - i + rc, lanes]
                cnt = jnp.minimum(k, pos + 1).astype(F32)
                d_scr[s * tc + r0:s * tc + r0 + rc, lanes] = (ws / cnt - cur).astype(BF16)
    for gi in range(len(POOL_WINDOWS)):
        lanes = slice(gi * group, (gi + 1) * group)
        y = _dot(d_scr[:, lanes], pw_ref[gi]) * ps_ref[:, lanes]
        ya_ref[:, lanes] = y.astype(ya_ref.dtype)
    for s in range(nseq):
        tail = p_scr[s, hp + tc - POOL_HIST:hp + tc, :]
        newc_ref[s] = tail
        p_scr[s, hp - POOL_HIST:hp, :] = tail


def _pool_call(x2d, g, w_in_bf, l, pool_w_bf, pool_scale, cache, *, n_seq, seq_len, tile_rows, pos0):
    rows, d_model = x2d.shape
    d_pool = pool_scale.shape[-1]
    ngroups = len(POOL_WINDOWS)
    group = d_pool // ngroups
    nseq, tc, chunks = _tile_geometry(n_seq, seq_len, tile_rows)
    grid = (rows // tile_rows,)
    kern = functools.partial(_pool_kernel, nseq=nseq, tc=tc, chunks=chunks, pos0=pos0, group=group)
    return pl.pallas_call(
        kern,
        grid=grid,
        in_specs=[
            pl.BlockSpec((tile_rows, d_model), lambda m: (m, 0)),
            pl.BlockSpec((None, 1, d_model), lambda m: (l, 0, 0)),
            pl.BlockSpec((None, d_model, d_pool), lambda m: (l, 0, 0)),
            pl.BlockSpec((None, ngroups, group, group), lambda m: (l, 0, 0, 0)),
            pl.BlockSpec((None, 1, d_pool), lambda m: (l, 0, 0)),
            pl.BlockSpec((None, nseq, POOL_HIST, d_pool), lambda m: (l, m // chunks, 0, 0)),
        ],
        out_specs=[
            pl.BlockSpec((tile_rows, d_pool), lambda m: (m, 0)),
            pl.BlockSpec((nseq, POOL_HIST, d_pool), lambda m: (m // chunks, 0, 0)),
        ],
        out_shape=[
            jax.ShapeDtypeStruct((rows, d_pool), BF16),
            jax.ShapeDtypeStruct((n_seq, POOL_HIST, d_pool), F32),
        ],
        scratch_shapes=[
            pltpu.VMEM((tile_rows, d_model), BF16),
            pltpu.VMEM((nseq, 2 * SUBLANES + tc, d_pool), F32),
            pltpu.VMEM((tile_rows, d_pool), BF16),
        ],
        compiler_params=pltpu.CompilerParams(
            dimension_semantics=("arbitrary",), vmem_limit_bytes=VMEM_LIMIT_BYTES_V7X),
        name="pool_mixer",
    )(x2d, g, w_in_bf, pool_w_bf, pool_scale, cache)


def _conv_kernel(x_ref, g_ref, wgb_ref, wgc_ref, wxb_ref, wcv_ref, wcg_ref, sw_ref, cw_ref, cb_ref,
                 scache_ref, ccache_ref, yb_ref, zc_ref, news_ref, newc_ref,
                 h_scr, q_pad, e_pad, s_carry, c_carry, *, nseq, tc, chunks, ks, kc):
    hq = _round_up(ks - 1, SUBLANES)
    he = _round_up(kc - 1, SUBLANES)
    m = pl.program_id(0)
    j = pl.program_id(1)

    @pl.when(j == 0)
    def _():
        _rmsnorm_rows(x_ref, g_ref, h_scr)
        if chunks > 1:
            @pl.when(m == 0)
            def _():
                s_carry[...] = jnp.zeros_like(s_carry)
                c_carry[...] = jnp.zeros_like(c_carry)

    def history(cache_ref, carry_ref, s):
        hist = cache_ref[s]
        if chunks > 1:
            first = (m % chunks == 0).astype(jnp.int32)
            hist = jnp.where(lax.broadcast(first, hist.shape) == 1, hist, carry_ref[j, s])
        return hist

    rc = min(ROW_CHUNK, tc)
    h = h_scr[...]

    glu = _dot(h, wcv_ref[...]) * jax.nn.sigmoid(_dot(h, wcg_ref[...]))
    for s in range(nseq):
        e_pad[s, he - (kc - 1):he, :] = history(ccache_ref, c_carry, s)
        e_pad[s, he:he + tc, :] = glu[s * tc:(s + 1) * tc, :]
        e_pad[s, he + tc:he + tc + SUBLANES, :] = jnp.zeros((SUBLANES, e_pad.shape[-1]), F32)
    def conv31_slab(s, r0):
        base = he - (kc - 1)
        zc = cb_ref[...]
        for a in range(SUBLANES):
            part = None
            for k in range(kc):
                if (base + k) % SUBLANES != a:
                    continue
                t0 = r0 + base + k - a
                term = cw_ref[k:k + 1, :] * e_pad[s, t0:t0 + rc + SUBLANES, :]
                part = term if part is None else part + term
            if part is not None:
                zc = zc + part[a:a + rc, :]
        zc_ref[s * tc + r0:s * tc + r0 + rc, :] = zc

    slabs = [(s, r0) for s in range(nseq) for r0 in range(0, tc, rc)]
    per_gap = -(-len(slabs) // 3)
    for sl in slabs[:per_gap]:
        conv31_slab(*sl)
    gate_c = _dot(h, wgc_ref[...])
    for sl in slabs[per_gap:2 * per_gap]:
        conv31_slab(*sl)
    x_b = _dot(h, wxb_ref[...])
    for sl in slabs[2 * per_gap:]:
        conv31_slab(*sl)
    gate_b = _dot(h, wgb_ref[...])
    for s in range(nseq):
        c_tail = e_pad[s, he + tc - (kc - 1):he + tc, :]
        newc_ref[s] = c_tail
        if chunks > 1:
            c_carry[j, s] = c_tail

    q = gate_c * x_b
    for s in range(nseq):
        q_pad[s, hq - (ks - 1):hq, :] = history(scache_ref, s_carry, s)
        q_pad[s, hq:hq + tc, :] = q[s * tc:(s + 1) * tc, :]
    for s in range(nseq):
        for r0 in range(0, tc, rc):
            rows = slice(s * tc + r0, s * tc + r0 + rc)
            zb = None
            for k in range(ks):
                t0 = hq - (ks - 1) + k + r0
                term = sw_ref[k:k + 1, :] * q_pad[s, t0:t0 + rc, :]
                zb = term if zb is None else zb + term
            yb_ref[rows, :] = (gate_b[rows, :] * zb).astype(yb_ref.dtype)
        s_tail = q_pad[s, hq + tc - (ks - 1):hq + tc, :]
        news_ref[s] = s_tail
        if chunks > 1:
            s_carry[j, s] = s_tail


def _conv_call(x2d, g, w_in_bf, l, sconv_w, cconv_w, cconv_b, scache, ccache, *, d_pool, n_seq, seq_len, tile_rows):
    rows, d_model = x2d.shape
    ks, d_conv = sconv_w.shape[-2:]
    kc = cconv_w.shape[-2]
    cb = CONV_BLOCK
    nj = d_conv // cb
    nseq, tc, chunks = _tile_geometry(n_seq, seq_len, tile_rows)
    n_tiles = rows // tile_rows
    hq = _round_up(ks - 1, SUBLANES)
    he = _round_up(kc - 1, SUBLANES)

    def w_spec(slot):
        off = (d_pool + slot * d_conv) // cb
        return pl.BlockSpec((None, d_model, cb), lambda m, j: (l, 0, off + j))

    kern = functools.partial(_conv_kernel, nseq=nseq, tc=tc, chunks=chunks, ks=ks, kc=kc)
    yb, zc, s_tails, c_tails = pl.pallas_call(
        kern,
        grid=(n_tiles, nj),
        in_specs=[
            pl.BlockSpec((tile_rows, d_model), lambda m, j: (m, 0)),
            pl.BlockSpec((None, 1, d_model), lambda m, j: (l, 0, 0)),
            w_spec(0), w_spec(1), w_spec(2), w_spec(3), w_spec(4),
            pl.BlockSpec((None, ks, cb), lambda m, j: (l, 0, j)),
            pl.BlockSpec((None, kc, cb), lambda m, j: (l, 0, j)),
            pl.BlockSpec((None, 1, cb), lambda m, j: (l, 0, j)),
            pl.BlockSpec((None, nseq, ks - 1, cb), lambda m, j: (l, m // chunks, 0, j)),
            pl.BlockSpec((None, nseq, kc - 1, cb), lambda m, j: (l, m // chunks, 0, j)),
        ],
        out_specs=[
            pl.BlockSpec((tile_rows, cb), lambda m, j: (m, j)),
            pl.BlockSpec((tile_rows, cb), lambda m, j: (m, j)),
            pl.BlockSpec((None, nseq, ks - 1, cb), lambda m, j: (m, 0, 0, j)),
            pl.BlockSpec((None, nseq, kc - 1, cb), lambda m, j: (m, 0, 0, j)),
        ],
        out_shape=[
            jax.ShapeDtypeStruct((rows, d_conv), BF16),
            jax.ShapeDtypeStruct((rows, d_conv), F32),
            jax.ShapeDtypeStruct((n_tiles, nseq, ks - 1, d_conv), F32),
            jax.ShapeDtypeStruct((n_tiles, nseq, kc - 1, d_conv), F32),
        ],
        scratch_shapes=[
            pltpu.VMEM((tile_rows, d_model), BF16),
            pltpu.VMEM((nseq, hq + tc, cb), F32),
            pltpu.VMEM((nseq, he + tc + SUBLANES, cb), F32),
            pltpu.VMEM((nj, nseq, ks - 1, cb), F32),
            pltpu.VMEM((nj, nseq, kc - 1, cb), F32),
        ],
        compiler_params=pltpu.CompilerParams(
            dimension_semantics=("arbitrary", "arbitrary"), vmem_limit_bytes=VMEM_LIMIT_BYTES_V7X),
        name="conv_mixers",
    )(x2d, g, w_in_bf, w_in_bf, w_in_bf, w_in_bf, w_in_bf, sconv_w, cconv_w, cconv_b, scache, ccache)
    new_sconv = s_tails[chunks - 1::chunks].reshape(n_seq, ks - 1, d_conv)
    new_cconv = c_tails[chunks - 1::chunks].reshape(n_seq, kc - 1, d_conv)
    return yb, zc, new_sconv, new_cconv


def _out_kernel(x_ref, ya_ref, yb_ref, zc_ref, lg_ref, lb_ref, w_ref, o_ref, y_scr, *, d_pool, d_conv):
    @pl.when(pl.program_id(1) == 0)
    def _():
        def body(rows):
            y_scr[rows, :d_pool] = ya_ref[rows, :]
            y_scr[rows, d_pool:d_pool + d_conv] = yb_ref[rows, :]
            z = zc_ref[rows, :]
            zc = z - jnp.mean(z, axis=-1, keepdims=True)
            var = jnp.mean(zc * zc, axis=-1, keepdims=True)
            ln = zc * lax.rsqrt(var + LN_EPS) * lg_ref[...] + lb_ref[...]
            y_scr[rows, d_pool + d_conv:] = (ln * jax.nn.sigmoid(ln)).astype(BF16)
        _for_row_chunks(y_scr.shape[0], body)

    o_ref[...] = x_ref[...] + _dot(y_scr[...], w_ref[...])


def _out_call(x2d, ya, yb, zc, cnorm_g, cnorm_b, w_out_bf, l, *, tile_rows, tile_cols):
    rows, d_model = x2d.shape
    d_pool, d_conv = ya.shape[-1], yb.shape[-1]
    d_mix = w_out_bf.shape[-2]
    grid = (rows // tile_rows, d_model // tile_cols)
    kern = functools.partial(_out_kernel, d_pool=d_pool, d_conv=d_conv)
    return pl.pallas_call(
        kern,
        grid=grid,
        in_specs=[
            pl.BlockSpec((tile_rows, tile_cols), lambda m, n: (m, n)),
            pl.BlockSpec((tile_rows, d_pool), lambda m, n: (m, 0)),
            pl.BlockSpec((tile_rows, d_conv), lambda m, n: (m, 0)),
            pl.BlockSpec((tile_rows, d_conv), lambda m, n: (m, 0)),
            pl.BlockSpec((None, 1, d_conv), lambda m, n: (l, 0, 0)),
            pl.BlockSpec((None, 1, d_conv), lambda m, n: (l, 0, 0)),
            pl.BlockSpec((None, d_mix, tile_cols), lambda m, n: (l, 0, n)),
        ],
        out_specs=pl.BlockSpec((tile_rows, tile_cols), lambda m, n: (m, n)),
        out_shape=jax.ShapeDtypeStruct((rows, d_model), F32),
        scratch_shapes=[pltpu.VMEM((tile_rows, d_mix), BF16)],
        compiler_params=pltpu.CompilerParams(
            dimension_semantics=("arbitrary", "arbitrary"), vmem_limit_bytes=VMEM_LIMIT_BYTES_V7X),
        name="out_proj",
    )(x2d, ya, yb, zc, cnorm_g, cnorm_b, w_out_bf)


def _ffn_kernel(x_ref, g_ref, wg0_ref, wu0_ref, wd0_ref, wg1_ref, wu1_ref, wd1_ref, gf_ref, o_ref, h_scr,
                *, final_norm, n_ff_blocks):
    i = pl.program_id(1)
    last = pl.num_programs(1) - 1

    @pl.when(i == 0)
    def _():
        def body(rows):
            h_scr[rows, :] = _rmsnorm_f32(x_ref[rows, :], g_ref[...]).astype(BF16)
            o_ref[rows, :] = jnp.zeros((NORM_ROWS, o_ref.shape[-1]), F32)
        _for_row_chunks(x_ref.shape[0], body)

    def ff_block(wg_ref, wu_ref, wd_ref):
        h = h_scr[...]
        a = _dot(h, wg_ref[...])
        b = _dot(h, wu_ref[...])
        act = (a * jax.nn.sigmoid(a) * b).astype(BF16)
        for c0 in range(0, o_ref.shape[-1], FFN_DOWN_COLS):
            cols = slice(c0, c0 + FFN_DOWN_COLS)
            o_ref[:, cols] += _dot(act, wd_ref[:, cols])

    if n_ff_blocks % 2 == 0:
        ff_block(wg0_ref, wu0_ref, wd0_ref)
        ff_block(wg1_ref, wu1_ref, wd1_ref)
    else:
        @pl.when(i < last)
        def _():
            ff_block(wg0_ref, wu0_ref, wd0_ref)
            ff_block(wg1_ref, wu1_ref, wd1_ref)

        @pl.when(i == last)
        def _():
            ff_block(wg0_ref, wu0_ref, wd0_ref)

    @pl.when(i == last)
    def _():
        def body(rows):
            y = x_ref[rows, :] + o_ref[rows, :]
            if final_norm:
                y = _rmsnorm_f32(y, gf_ref[...])
            o_ref[rows, :] = y
        _for_row_chunks(x_ref.shape[0], body)


def _ffn_call(x2d, g, w_gate_bf, w_up_bf, w_down_bf, l, norm_final, *, tile_rows, tile_ff, final_norm):
    rows, d_model = x2d.shape
    d_ff = w_gate_bf.shape[-1]
    assert d_ff % tile_ff == 0
    n_ff_blocks = d_ff // tile_ff
    grid = (rows // tile_rows, -(-n_ff_blocks // 2))
    kern = functools.partial(_ffn_kernel, final_norm=final_norm, n_ff_blocks=n_ff_blocks)

    def w_specs(second):
        f = lambda i: jnp.minimum(2 * i + second, n_ff_blocks - 1)
        return [pl.BlockSpec((None, d_model, tile_ff), lambda m, i: (l, 0, f(i))),
                pl.BlockSpec((None, d_model, tile_ff), lambda m, i: (l, 0, f(i))),
                pl.BlockSpec((None, tile_ff, d_model), lambda m, i: (l, f(i), 0))]

    return pl.pallas_call(
        kern,
        grid=grid,
        in_specs=[
            pl.BlockSpec((tile_rows, d_model), lambda m, i: (m, 0), pipeline_mode=pl.Buffered(1)),
            pl.BlockSpec((None, 1, d_model), lambda m, i: (l, 0, 0)),
            *w_specs(0), *w_specs(1),
            pl.BlockSpec((1, d_model), lambda m, i: (0, 0)),
        ],
        out_specs=pl.BlockSpec((tile_rows, d_model), lambda m, i: (m, 0)),
        out_shape=jax.ShapeDtypeStruct((rows, d_model), F32),
        scratch_shapes=[pltpu.VMEM((tile_rows, d_model), BF16)],
        compiler_params=pltpu.CompilerParams(
            dimension_semantics=("arbitrary", "arbitrary"), vmem_limit_bytes=VMEM_LIMIT_BYTES_V7X),
        name="swiglu",
    )(x2d, g, w_gate_bf, w_up_bf, w_down_bf, w_gate_bf, w_up_bf, w_down_bf, norm_final)


def _trunk(x, cache_pool, cache_sconv, cache_cconv, pos0, p, *, tile_rows):
    n_seq, seq_len, d_model = x.shape
    depth = p["w_in"].shape[0]
    d_pool = p["pool_scale"].shape[-1]
    x2d = x.reshape(n_seq * seq_len, d_model)
    geo = dict(n_seq=n_seq, seq_len=seq_len, tile_rows=tile_rows)
    pools, sconvs, cconvs = [], [], []
    for l in range(depth):
        ya, new_pool = _pool_call(x2d, p["norm_mix"], p["w_in"], l, p["pool_w"], p["pool_scale"], cache_pool,
                                  pos0=pos0, **geo)
        yb, zc, new_sconv, new_cconv = _conv_call(x2d, p["norm_mix"], p["w_in"], l, p["sconv_w"], p["cconv_w"],
                                                  p["cconv_b"], cache_sconv, cache_cconv, d_pool=d_pool, **geo)
        x2d = _out_call(x2d, ya, yb, zc, p["cnorm_g"], p["cnorm_b"], p["w_out"], l,
                        tile_rows=tile_rows, tile_cols=1024)
        x2d = _ffn_call(x2d, p["norm_ffn"], p["w_gate"], p["w_up"], p["w_down"], l, p["norm_final"],
                        tile_rows=tile_rows, tile_ff=256, final_norm=(l == depth - 1))
        pools.append(new_pool)
        sconvs.append(new_sconv)
        cconvs.append(new_cconv)
    return x2d.reshape(x.shape), jnp.stack(pools), jnp.stack(sconvs), jnp.stack(cconvs)


def kernel(x_prompt, x_sample, cache_pool, cache_sconv, cache_cconv, norm_mix, norm_ffn, w_in, pool_w, pool_scale,
           sconv_w, cconv_w, cconv_b, cnorm_g, cnorm_b, w_out, w_gate, w_up, w_down, norm_final):
    depth, d_model = norm_mix.shape
    batch = x_prompt.shape[0]
    dt = x_prompt.dtype
    row = lambda a: a.reshape(a.shape[0], 1, a.shape[-1])
    p = dict(
        norm_mix=row(norm_mix), norm_ffn=row(norm_ffn), w_in=w_in.astype(BF16), pool_w=pool_w.astype(BF16),
        pool_scale=row(pool_scale), sconv_w=sconv_w, cconv_w=cconv_w, cconv_b=row(cconv_b), cnorm_g=row(cnorm_g),
        cnorm_b=row(cnorm_b), w_out=w_out.astype(BF16), w_gate=w_gate.astype(BF16), w_up=w_up.astype(BF16),
        w_down=w_down.astype(BF16), norm_final=norm_final.reshape(1, d_model))
    zero_pool = jnp.zeros((depth, batch) + cache_pool.shape[2:], dt)
    zero_sconv = jnp.zeros((depth, batch) + cache_sconv.shape[2:], dt)
    zero_cconv = jnp.zeros((depth, batch) + cache_cconv.shape[2:], dt)
    y_p, pool_p, sconv_p, cconv_p = _trunk(x_prompt, zero_pool, zero_sconv, zero_cconv, 0, p, tile_rows=512)
    y_s, pool_s, sconv_s, cconv_s = _trunk(x_sample, cache_pool, cache_sconv, cache_cconv, PAST_LEN, p,
                                           tile_rows=512)
    return (y_p, y_s, pool_p, pool_s, sconv_p, sconv_s, cconv_p, cconv_s)
```

```python
import functools

import jax
import jax.numpy as jnp
from jax import lax
from jax.experimental import pallas as pl
from jax.experimental.pallas import tpu as pltpu

POOL_WINDOWS = (2, 4, 8, 16)
POOL_HIST = max(POOL_WINDOWS) - 1
PAST_LEN = 2048
RMS_EPS = 1e-6
LN_EPS = 1e-5

VMEM_LIMIT_BYTES_V7X = 56 * 1024 * 1024
SUBLANES = 8
NORM_ROWS = 16
NORM_UNROLL = 8
ROW_CHUNK = 64
ROW_TILE_SMALL = 512
ROW_TILE_LARGE = 1024
OUT_TILE_COLS = 512
FFN_TILE = 256
CONV_BLOCK = 256
FFN_DOWN_COLS = 1024

F32 = jnp.float32
BF16 = jnp.bfloat16


def _round_up(n, m):
    return -(-n // m) * m


def _rmsnorm_f32(x, g):
    return x * lax.rsqrt(jnp.mean(x * x, axis=-1, keepdims=True) + RMS_EPS) * g


def _for_row_chunks(n_rows, body):
    def step(i, carry):
        body(pl.ds(pl.multiple_of(i * NORM_ROWS, NORM_ROWS), NORM_ROWS))
        return carry
    lax.fori_loop(0, n_rows // NORM_ROWS, step, 0, unroll=NORM_UNROLL)


def _rmsnorm_rows(x_ref, g_ref, dst_ref):
    def body(rows):
        dst_ref[rows, :] = _rmsnorm_f32(x_ref[rows, :], g_ref[...]).astype(dst_ref.dtype)
    _for_row_chunks(x_ref.shape[0], body)


def _dot(a, b):
    return jnp.dot(a, b, preferred_element_type=F32)


def _tile_geometry(n_seq_total, seq_len, tile_rows):
    if tile_rows >= seq_len:
        assert tile_rows % seq_len == 0
        nseq, tc, chunks = tile_rows // seq_len, seq_len, 1
    else:
        assert seq_len % tile_rows == 0
        nseq, tc, chunks = 1, tile_rows, seq_len // tile_rows
    assert n_seq_total % nseq == 0
    return nseq, tc, chunks


def _pool_kernel(x_ref, g_ref, w_ref, pw_ref, ps_ref, cache_ref, ya_ref, newc_ref, h_ref, p_scr, d_scr,
                 *, nseq, tc, chunks, pos0, group):
    hp = 2 * SUBLANES
    chunk = pl.program_id(0) % chunks
    _rmsnorm_rows(x_ref, g_ref, h_ref)
    v = _dot(h_ref[...], w_ref[...])
    for s in range(nseq):
        @pl.when(chunk == 0)
        def _():
            p_scr[s, hp - POOL_HIST:hp, :] = cache_ref[s]
        p_scr[s, hp:hp + tc, :] = v[s * tc:(s + 1) * tc, :]
    rc = min(ROW_CHUNK, tc)
    for s in range(nseq):
        for r0 in range(0, tc, rc):
            pos = pos0 + chunk * tc + r0 + lax.broadcasted_iota(jnp.int32, (rc, group), 0)
            for gi, k in enumerate(POOL_WINDOWS):
                lanes = slice(gi * group, (gi + 1) * group)
                cur = p_scr[s, hp + r0:hp + r0 + rc, lanes]
                ws = cur
                for i in range(1, k):
                    ws = ws + p_scr[s, hp + r0 - i:hp + r0 - i + rc, lanes]
                cnt = jnp.minimum(k, pos + 1).astype(F32)
                d_scr[s * tc + r0:s * tc + r0 + rc, lanes] = (ws / cnt - cur).astype(BF16)
    for gi in range(len(POOL_WINDOWS)):
        lanes = slice(gi * group, (gi + 1) * group)
        y = _dot(d_scr[:, lanes], pw_ref[gi]) * ps_ref[:, lanes]
        ya_ref[:, lanes] = y.astype(ya_ref.dtype)
    for s in range(nseq):
        tail = p_scr[s, hp + tc - POOL_HIST:hp + tc, :]
        newc_ref[s] = tail
        p_scr[s, hp - POOL_HIST:hp, :] = tail


def _pool_call(x2d, g, w_in_bf, l, pool_w_bf, pool_scale, cache, *, n_seq, seq_len, tile_rows, pos0):
    rows, d_model = x2d.shape
    d_pool = pool_scale.shape[-1]
    ngroups = len(POOL_WINDOWS)
    group = d_pool // ngroups
    nseq, tc, chunks = _tile_geometry(n_seq, seq_len, tile_rows)
    grid = (rows // tile_rows,)
    kern = functools.partial(_pool_kernel, nseq=nseq, tc=tc, chunks=chunks, pos0=pos0, group=group)
    return pl.pallas_call(
        kern,
        grid=grid,
        in_specs=[
            pl.BlockSpec((tile_rows, d_model), lambda m: (m, 0)),
            pl.BlockSpec((None, 1, d_model), lambda m: (l, 0, 0)),
            pl.BlockSpec((None, d_model, d_pool), lambda m: (l, 0, 0)),
            pl.BlockSpec((None, ngroups, group, group), lambda m: (l, 0, 0, 0)),
            pl.BlockSpec((None, 1, d_pool), lambda m: (l, 0, 0)),
            pl.BlockSpec((None, nseq, POOL_HIST, d_pool), lambda m: (l, m // chunks, 0, 0)),
        ],
        out_specs=[
            pl.BlockSpec((tile_rows, d_pool), lambda m: (m, 0)),
            pl.BlockSpec((nseq, POOL_HIST, d_pool), lambda m: (m // chunks, 0, 0)),
            pl.BlockSpec((tile_rows, d_model), lambda m: (m, 0)),
        ],
        out_shape=[
            jax.ShapeDtypeStruct((rows, d_pool), BF16),
            jax.ShapeDtypeStruct((n_seq, POOL_HIST, d_pool), F32),
            jax.ShapeDtypeStruct((rows, d_model), BF16),
        ],
        scratch_shapes=[
            pltpu.VMEM((nseq, 2 * SUBLANES + tc, d_pool), F32),
            pltpu.VMEM((tile_rows, d_pool), BF16),
        ],
        compiler_params=pltpu.CompilerParams(
            dimension_semantics=("arbitrary",), vmem_limit_bytes=VMEM_LIMIT_BYTES_V7X),
        name="pool_mixer",
    )(x2d, g, w_in_bf, pool_w_bf, pool_scale, cache)


def _conv_kernel(h_ref, wgb_ref, wgc_ref, wxb_ref, wcv_ref, wcg_ref, sw_ref, cw_ref, cb_ref,
                 scache_ref, ccache_ref, yb_ref, zc_ref, news_ref, newc_ref,
                 q_pad, e_pad, s_carry, c_carry, *, nseq, tc, chunks, ks, kc):
    hq = _round_up(ks - 1, SUBLANES)
    he = _round_up(kc - 1, SUBLANES)
    m = pl.program_id(0)
    j = pl.program_id(1)

    if chunks > 1:
        @pl.when((m == 0) & (j == 0))
        def _():
            s_carry[...] = jnp.zeros_like(s_carry)
            c_carry[...] = jnp.zeros_like(c_carry)

    def history(cache_ref, carry_ref, s):
        hist = cache_ref[s]
        if chunks > 1:
            first = (m % chunks == 0).astype(jnp.int32)
            hist = jnp.where(lax.broadcast(first, hist.shape) == 1, hist, carry_ref[j, s])
        return hist

    rc = min(ROW_CHUNK, tc)
    h = h_ref[...]

    glu = _dot(h, wcv_ref[...]) * jax.nn.sigmoid(_dot(h, wcg_ref[...]))
    for s in range(nseq):
        e_pad[s, he - (kc - 1):he, :] = history(ccache_ref, c_carry, s)
        e_pad[s, he:he + tc, :] = glu[s * tc:(s + 1) * tc, :]
        e_pad[s, he + tc:he + tc + SUBLANES, :] = jnp.zeros((SUBLANES, e_pad.shape[-1]), F32)
    def conv31_slab(s, r0):
        base = he - (kc - 1)
        zc = cb_ref[...]
        for a in range(SUBLANES):
            part = None
            for k in range(kc):
                if (base + k) % SUBLANES != a:
                    continue
                t0 = r0 + base + k - a
                term = cw_ref[k:k + 1, :] * e_pad[s, t0:t0 + rc + SUBLANES, :]
                part = term if part is None else part + term
            if part is not None:
                zc = zc + part[a:a + rc, :]
        zc_ref[s * tc + r0:s * tc + r0 + rc, :] = zc

    slabs = [(s, r0) for s in range(nseq) for r0 in range(0, tc, rc)]
    per_gap = -(-len(slabs) // 3)
    for sl in slabs[:per_gap]:
        conv31_slab(*sl)
    gate_c = _dot(h, wgc_ref[...])
    for sl in slabs[per_gap:2 * per_gap]:
        conv31_slab(*sl)
    x_b = _dot(h, wxb_ref[...])
    for sl in slabs[2 * per_gap:]:
        conv31_slab(*sl)
    gate_b = _dot(h, wgb_ref[...])
    for s in range(nseq):
        c_tail = e_pad[s, he + tc - (kc - 1):he + tc, :]
        newc_ref[s] = c_tail
        if chunks > 1:
            c_carry[j, s] = c_tail

    q = gate_c * x_b
    for s in range(nseq):
        q_pad[s, hq - (ks - 1):hq, :] = history(scache_ref, s_carry, s)
        q_pad[s, hq:hq + tc, :] = q[s * tc:(s + 1) * tc, :]
    for s in range(nseq):
        for r0 in range(0, tc, rc):
            rows = slice(s * tc + r0, s * tc + r0 + rc)
            zb = None
            for k in range(ks):
                t0 = hq - (ks - 1) + k + r0
                term = sw_ref[k:k + 1, :] * q_pad[s, t0:t0 + rc, :]
                zb = term if zb is None else zb + term
            yb_ref[rows, :] = (gate_b[rows, :] * zb).astype(yb_ref.dtype)
        s_tail = q_pad[s, hq + tc - (ks - 1):hq + tc, :]
        news_ref[s] = s_tail
        if chunks > 1:
            s_carry[j, s] = s_tail


def _conv_call(h2d, w_in_bf, l, sconv_w, cconv_w, cconv_b, scache, ccache, *, d_pool, n_seq, seq_len, tile_rows):
    rows, d_model = h2d.shape
    ks, d_conv = sconv_w.shape[-2:]
    kc = cconv_w.shape[-2]
    cb = CONV_BLOCK
    nj = d_conv // cb
    nseq, tc, chunks = _tile_geometry(n_seq, seq_len, tile_rows)
    n_tiles = rows // tile_rows
    hq = _round_up(ks - 1, SUBLANES)
    he = _round_up(kc - 1, SUBLANES)

    def w_spec(slot):
        off = (d_pool + slot * d_conv) // cb
        return pl.BlockSpec((None, d_model, cb), lambda m, j: (l, 0, off + j))

    kern = functools.partial(_conv_kernel, nseq=nseq, tc=tc, chunks=chunks, ks=ks, kc=kc)
    yb, zc, s_tails, c_tails = pl.pallas_call(
        kern,
        grid=(n_tiles, nj),
        in_specs=[
            pl.BlockSpec((tile_rows, d_model), lambda m, j: (m, 0)),
            w_spec(0), w_spec(1), w_spec(2), w_spec(3), w_spec(4),
            pl.BlockSpec((None, ks, cb), lambda m, j: (l, 0, j)),
            pl.BlockSpec((None, kc, cb), lambda m, j: (l, 0, j)),
            pl.BlockSpec((None, 1, cb), lambda m, j: (l, 0, j)),
            pl.BlockSpec((None, nseq, ks - 1, cb), lambda m, j: (l, m // chunks, 0, j)),
            pl.BlockSpec((None, nseq, kc - 1, cb), lambda m, j: (l, m // chunks, 0, j)),
        ],
        out_specs=[
            pl.BlockSpec((tile_rows, cb), lambda m, j: (m, j)),
            pl.BlockSpec((tile_rows, cb), lambda m, j: (m, j)),
            pl.BlockSpec((None, nseq, ks - 1, cb), lambda m, j: (m, 0, 0, j)),
            pl.BlockSpec((None, nseq, kc - 1, cb), lambda m, j: (m, 0, 0, j)),
        ],
        out_shape=[
            jax.ShapeDtypeStruct((rows, d_conv), BF16),
            jax.ShapeDtypeStruct((rows, d_conv), F32),
            jax.ShapeDtypeStruct((n_tiles, nseq, ks - 1, d_conv), F32),
            jax.ShapeDtypeStruct((n_tiles, nseq, kc - 1, d_conv), F32),
        ],
        scratch_shapes=[
            pltpu.VMEM((nseq, hq + tc, cb), F32),
            pltpu.VMEM((nseq, he + tc + SUBLANES, cb), F32),
            pltpu.VMEM((nj, nseq, ks - 1, cb), F32),
            pltpu.VMEM((nj, nseq, kc - 1, cb), F32),
        ],
        compiler_params=pltpu.CompilerParams(
            dimension_semantics=("arbitrary", "arbitrary"), vmem_limit_bytes=VMEM_LIMIT_BYTES_V7X),
        name="conv_mixers",
    )(h2d, w_in_bf, w_in_bf, w_in_bf, w_in_bf, w_in_bf, sconv_w, cconv_w, cconv_b, scache, ccache)
    new_sconv = s_tails[chunks - 1::chunks].reshape(n_seq, ks - 1, d_conv)
    new_cconv = c_tails[chunks - 1::chunks].reshape(n_seq, kc - 1, d_conv)
    return yb, zc, new_sconv, new_cconv


def _out_kernel(x_ref, ya_ref, yb_ref, zc_ref, lg_ref, lb_ref, w_ref, o_ref, y_scr, *, d_pool, d_conv):
    @pl.when(pl.program_id(1) == 0)
    def _():
        def body(rows):
            y_scr[rows, :d_pool] = ya_ref[rows, :]
            y_scr[rows, d_pool:d_pool + d_conv] = yb_ref[rows, :]
            z = zc_ref[rows, :]
            zc = z - jnp.mean(z, axis=-1, keepdims=True)
            var = jnp.mean(zc * zc, axis=-1, keepdims=True)
            ln = zc * lax.rsqrt(var + LN_EPS) * lg_ref[...] + lb_ref[...]
            y_scr[rows, d_pool + d_conv:] = (ln * jax.nn.sigmoid(ln)).astype(BF16)
        _for_row_chunks(y_scr.shape[0], body)

    o_ref[...] = x_ref[...] + _dot(y_scr[...], w_ref[...])


def _out_call(x2d, ya, yb, zc, cnorm_g, cnorm_b, w_out_bf, l, *, tile_rows, tile_cols):
    rows, d_model = x2d.shape
    d_pool, d_conv = ya.shape[-1], yb.shape[-1]
    d_mix = w_out_bf.shape[-2]
    grid = (rows // tile_rows, d_model // tile_cols)
    kern = functools.partial(_out_kernel, d_pool=d_pool, d_conv=d_conv)
    return pl.pallas_call(
        kern,
        grid=grid,
        in_specs=[
            pl.BlockSpec((tile_rows, tile_cols), lambda m, n: (m, n)),
            pl.BlockSpec((tile_rows, d_pool), lambda m, n: (m, 0)),
            pl.BlockSpec((tile_rows, d_conv), lambda m, n: (m, 0)),
            pl.BlockSpec((tile_rows, d_conv), lambda m, n: (m, 0)),
            pl.BlockSpec((None, 1, d_conv), lambda m, n: (l, 0, 0)),
            pl.BlockSpec((None, 1, d_conv), lambda m, n: (l, 0, 0)),
            pl.BlockSpec((None, d_mix, tile_cols), lambda m, n: (l, 0, n)),
        ],
        out_specs=pl.BlockSpec((tile_rows, tile_cols), lambda m, n: (m, n)),
        out_shape=jax.ShapeDtypeStruct((rows, d_model), F32),
        scratch_shapes=[pltpu.VMEM((tile_rows, d_mix), BF16)],
        compiler_params=pltpu.CompilerParams(
            dimension_semantics=("arbitrary", "arbitrary"), vmem_limit_bytes=VMEM_LIMIT_BYTES_V7X),
        name="out_proj",
    )(x2d, ya, yb, zc, cnorm_g, cnorm_b, w_out_bf)


def _ffn_kernel(x_ref, g_ref, wg0_ref, wu0_ref, wd0_ref, wg1_ref, wu1_ref, wd1_ref, gf_ref, o_ref, h_scr,
                *, final_norm, n_ff_blocks):
    i = pl.program_id(1)
    last = pl.num_programs(1) - 1

    @pl.when(i == 0)
    def _():
        def body(rows):
            h_scr[rows, :] = _rmsnorm_f32(x_ref[rows, :], g_ref[...]).astype(BF16)
            o_ref[rows, :] = jnp.zeros((NORM_ROWS, o_ref.shape[-1]), F32)
        _for_row_chunks(x_ref.shape[0], body)

    def ff_block(wg_ref, wu_ref, wd_ref):
        h = h_scr[...]
        a = _dot(h, wg_ref[...])
        b = _dot(h, wu_ref[...])
        act = (a * jax.nn.sigmoid(a) * b).astype(BF16)
        for c0 in range(0, o_ref.shape[-1], FFN_DOWN_COLS):
            cols = slice(c0, c0 + FFN_DOWN_COLS)
            o_ref[:, cols] += _dot(act, wd_ref[:, cols])

    if n_ff_blocks % 2 == 0:
        ff_block(wg0_ref, wu0_ref, wd0_ref)
        ff_block(wg1_ref, wu1_ref, wd1_ref)
    else:
        @pl.when(i < last)
        def _():
            ff_block(wg0_ref, wu0_ref, wd0_ref)
            ff_block(wg1_ref, wu1_ref, wd1_ref)

        @pl.when(i == last)
        def _():
            ff_block(wg0_ref, wu0_ref, wd0_ref)

    @pl.when(i == last)
    def _():
        def body(rows):
            y = x_ref[rows, :] + o_ref[rows, :]
            if final_norm:
                y = _rmsnorm_f32(y, gf_ref[...])
            o_ref[rows, :] = y
        _for_row_chunks(x_ref.shape[0], body)


def _ffn_call(x2d, g, w_gate_bf, w_up_bf, w_down_bf, l, norm_final, *, tile_rows, tile_ff, final_norm):
    rows, d_model = x2d.shape
    d_ff = w_gate_bf.shape[-1]
    assert d_ff % tile_ff == 0
    n_ff_blocks = d_ff // tile_ff
    grid = (rows // tile_rows, -(-n_ff_blocks // 2))
    kern = functools.partial(_ffn_kernel, final_norm=final_norm, n_ff_blocks=n_ff_blocks)

    def w_specs(second):
        f = lambda i: jnp.minimum(2 * i + second, n_ff_blocks - 1)
        return [pl.BlockSpec((None, d_model, tile_ff), lambda m, i: (l, 0, f(i))),
                pl.BlockSpec((None, d_model, tile_ff), lambda m, i: (l, 0, f(i))),
                pl.BlockSpec((None, tile_ff, d_model), lambda m, i: (l, f(i), 0))]

    return pl.pallas_call(
        kern,
        grid=grid,
        in_specs=[
            pl.BlockSpec((tile_rows, d_model), lambda m, i: (m, 0), pipeline_mode=pl.Buffered(1)),
            pl.BlockSpec((None, 1, d_model), lambda m, i: (l, 0, 0)),
            *w_specs(0), *w_specs(1),
            pl.BlockSpec((1, d_model), lambda m, i: (0, 0)),
        ],
        out_specs=pl.BlockSpec((tile_rows, d_model), lambda m, i: (m, 0)),
        out_shape=jax.ShapeDtypeStruct((rows, d_model), F32),
        scratch_shapes=[pltpu.VMEM((tile_rows, d_model), BF16)],
        compiler_params=pltpu.CompilerParams(
            dimension_semantics=("arbitrary", "arbitrary"), vmem_limit_bytes=VMEM_LIMIT_BYTES_V7X),
        name="swiglu",
    )(x2d, g, w_gate_bf, w_up_bf, w_down_bf, w_gate_bf, w_up_bf, w_down_bf, norm_final)


def _row_tiles(rows):
    small = min(ROW_TILE_SMALL, rows)
    large = min(ROW_TILE_LARGE, rows)
    return dict(pool=small, ffn=small, conv=large, out=large)


def _trunk(x, cache_pool, cache_sconv, cache_cconv, pos0, p):
    n_seq, seq_len, d_model = x.shape
    depth = p["w_in"].shape[0]
    d_pool = p["pool_scale"].shape[-1]
    x2d = x.reshape(n_seq * seq_len, d_model)
    tiles = _row_tiles(x2d.shape[0])
    geo = dict(n_seq=n_seq, seq_len=seq_len)
    pools, sconvs, cconvs = [], [], []
    for l in range(depth):
        ya, new_pool, h2d = _pool_call(x2d, p["norm_mix"], p["w_in"], l, p["pool_w"], p["pool_scale"], cache_pool,
                                       pos0=pos0, tile_rows=tiles["pool"], **geo)
        yb, zc, new_sconv, new_cconv = _conv_call(h2d, p["w_in"], l, p["sconv_w"], p["cconv_w"], p["cconv_b"],
                                                  cache_sconv, cache_cconv, d_pool=d_pool,
                                                  tile_rows=tiles["conv"], **geo)
        x2d = _out_call(x2d, ya, yb, zc, p["cnorm_g"], p["cnorm_b"], p["w_out"], l,
                        tile_rows=tiles["out"], tile_cols=OUT_TILE_COLS)
        x2d = _ffn_call(x2d, p["norm_ffn"], p["w_gate"], p["w_up"], p["w_down"], l, p["norm_final"],
                        tile_rows=tiles["ffn"], tile_ff=FFN_TILE, final_norm=(l == depth - 1))
        pools.append(new_pool)
        sconvs.append(new_sconv)
        cconvs.append(new_cconv)
    return x2d.reshape(x.shape), jnp.stack(pools), jnp.stack(sconvs), jnp.stack(cconvs)


def kernel(x_prompt, x_sample, cache_pool, cache_sconv, cache_cconv, norm_mix, norm_ffn, w_in, pool_w, pool_scale,
           sconv_w, cconv_w, cconv_b, cnorm_g, cnorm_b, w_out, w_gate, w_up, w_down, norm_final):
    depth, d_model = norm_mix.shape
    batch = x_prompt.shape[0]
    dt = x_prompt.dtype
    row = lambda a: a.reshape(a.shape[0], 1, a.shape[-1])
    p = dict(
        norm_mix=row(norm_mix), norm_ffn=row(norm_ffn), w_in=w_in.astype(BF16), pool_w=pool_w.astype(BF16),
        pool_scale=row(pool_scale), sconv_w=sconv_w, cconv_w=cconv_w, cconv_b=row(cconv_b), cnorm_g=row(cnorm_g),
        cnorm_b=row(cnorm_b), w_out=w_out.astype(BF16), w_gate=w_gate.astype(BF16), w_up=w_up.astype(BF16),
        w_down=w_down.astype(BF16), norm_final=norm_final.reshape(1, d_model))
    zero_pool = jnp.zeros((depth, batch) + cache_pool.shape[2:], dt)
    zero_sconv = jnp.zeros((depth, batch) + cache_sconv.shape[2:], dt)
    zero_cconv = jnp.zeros((depth, batch) + cache_cconv.shape[2:], dt)
    y_p, pool_p, sconv_p, cconv_p = _trunk(x_prompt, zero_pool, zero_sconv, zero_cconv, 0, p)
    y_s, pool_s, sconv_s, cconv_s = _trunk(x_sample, cache_pool, cache_sconv, cache_cconv, PAST_LEN, p)
    return (y_p, y_s, pool_p, pool_s, sconv_p, sconv_s, cconv_p, cconv_s)
```

```python
import functools

import jax
import jax.numpy as jnp
from jax import lax
from jax.experimental import pallas as pl
from jax.experimental.pallas import tpu as pltpu

POOL_WINDOWS = (2, 4, 8, 16)
POOL_HIST = max(POOL_WINDOWS) - 1
PAST_LEN = 2048
RMS_EPS = 1e-6
LN_EPS = 1e-5

VMEM_LIMIT_BYTES_V7X = 56 * 1024 * 1024
SUBLANES = 8
NORM_ROWS = 16
NORM_UNROLL = 8
ROW_CHUNK = 64
ROW_TILE_SMALL = 512
ROW_TILE_LARGE = 1024
OUT_TILE_COLS = 512
FFN_TILE = 256
CONV_BLOCK = 256
FFN_DOWN_COLS = 1024

F32 = jnp.float32
BF16 = jnp.bfloat16


def _round_up(n, m):
    return -(-n // m) * m


def _rmsnorm_f32(x, g):
    return x * lax.rsqrt(jnp.mean(x * x, axis=-1, keepdims=True) + RMS_EPS) * g


def _for_row_chunks(n_rows, body):
    def step(i, carry):
        body(pl.ds(pl.multiple_of(i * NORM_ROWS, NORM_ROWS), NORM_ROWS))
        return carry
    lax.fori_loop(0, n_rows // NORM_ROWS, step, 0, unroll=NORM_UNROLL)


def _rmsnorm_rows(x_ref, g_ref, dst_ref):
    def body(rows):
        dst_ref[rows, :] = _rmsnorm_f32(x_ref[rows, :], g_ref[...]).astype(dst_ref.dtype)
    _for_row_chunks(x_ref.shape[0], body)


def _dot(a, b):
    return jnp.dot(a, b, preferred_element_type=F32)


def _cast_block_kernel(w_ref, o_ref):
    o_ref[...] = w_ref[...].astype(o_ref.dtype)


def _column_blocks(w, width, first_col=0):
    depth, k, n = w.shape
    assert first_col % width == 0 and n % width == 0
    first_block = first_col // width
    n_blocks = n // width - first_block
    return pl.pallas_call(
        _cast_block_kernel,
        grid=(depth, n_blocks),
        in_specs=[pl.BlockSpec((None, k, width), lambda d, b: (d, 0, first_block + b))],
        out_specs=pl.BlockSpec((None, None, k, width), lambda d, b: (d, b, 0, 0)),
        out_shape=jax.ShapeDtypeStruct((depth, n_blocks, k, width), BF16),
        compiler_params=pltpu.CompilerParams(
            dimension_semantics=("arbitrary", "arbitrary"), vmem_limit_bytes=VMEM_LIMIT_BYTES_V7X),
        name="cast_column_blocks",
    )(w)


def _tile_geometry(n_seq_total, seq_len, tile_rows):
    if tile_rows >= seq_len:
        assert tile_rows % seq_len == 0
        nseq, tc, chunks = tile_rows // seq_len, seq_len, 1
    else:
        assert seq_len % tile_rows == 0
        nseq, tc, chunks = 1, tile_rows, seq_len // tile_rows
    assert n_seq_total % nseq == 0
    return nseq, tc, chunks


def _pool_kernel(x_ref, g_ref, w_ref, pw_ref, ps_ref, cache_ref, ya_ref, newc_ref, h_ref, p_scr, d_scr,
                 *, nseq, tc, chunks, pos0, group):
    hp = 2 * SUBLANES
    chunk = pl.program_id(0) % chunks
    _rmsnorm_rows(x_ref, g_ref, h_ref)
    v = _dot(h_ref[...], w_ref[...])
    for s in range(nseq):
        @pl.when(chunk == 0)
        def _():
            p_scr[s, hp - POOL_HIST:hp, :] = cache_ref[s]
        p_scr[s, hp:hp + tc, :] = v[s * tc:(s + 1) * tc, :]
    rc = min(ROW_CHUNK, tc)
    for s in range(nseq):
        for r0 in range(0, tc, rc):
            pos = pos0 + chunk * tc + r0 + lax.broadcasted_iota(jnp.int32, (rc, group), 0)
            for gi, k in enumerate(POOL_WINDOWS):
                lanes = slice(gi * group, (gi + 1) * group)
                cur = p_scr[s, hp + r0:hp + r0 + rc, lanes]
                ws = cur
                for i in range(1, k):
                    ws = ws + p_scr[s, hp + r0 - i:hp + r0 - i + rc, lanes]
                cnt = jnp.minimum(k, pos + 1).astype(F32)
                d_scr[s * tc + r0:s * tc + r0 + rc, lanes] = (ws / cnt - cur).astype(BF16)
    for gi in range(len(POOL_WINDOWS)):
        lanes = slice(gi * group, (gi + 1) * group)
        y = _dot(d_scr[:, lanes], pw_ref[gi]) * ps_ref[:, lanes]
        ya_ref[:, lanes] = y.astype(ya_ref.dtype)
    for s in range(nseq):
        tail = p_scr[s, hp + tc - POOL_HIST:hp + tc, :]
        newc_ref[s] = tail
        p_scr[s, hp - POOL_HIST:hp, :] = tail


def _pool_call(x2d, g, w_in_bf, l, pool_w_bf, pool_scale, cache, *, n_seq, seq_len, tile_rows, pos0):
    rows, d_model = x2d.shape
    d_pool = pool_scale.shape[-1]
    ngroups = len(POOL_WINDOWS)
    group = d_pool // ngroups
    nseq, tc, chunks = _tile_geometry(n_seq, seq_len, tile_rows)
    grid = (rows // tile_rows,)
    kern = functools.partial(_pool_kernel, nseq=nseq, tc=tc, chunks=chunks, pos0=pos0, group=group)
    return pl.pallas_call(
        kern,
        grid=grid,
        in_specs=[
            pl.BlockSpec((tile_rows, d_model), lambda m: (m, 0)),
            pl.BlockSpec((None, 1, d_model), lambda m: (l, 0, 0)),
            pl.BlockSpec((None, d_model, d_pool), lambda m: (l, 0, 0)),
            pl.BlockSpec((None, ngroups, group, group), lambda m: (l, 0, 0, 0)),
            pl.BlockSpec((None, 1, d_pool), lambda m: (l, 0, 0)),
            pl.BlockSpec((None, nseq, POOL_HIST, d_pool), lambda m: (l, m // chunks, 0, 0)),
        ],
        out_specs=[
            pl.BlockSpec((tile_rows, d_pool), lambda m: (m, 0)),
            pl.BlockSpec((nseq, POOL_HIST, d_pool), lambda m: (m // chunks, 0, 0)),
            pl.BlockSpec((tile_rows, d_model), lambda m: (m, 0)),
        ],
        out_shape=[
            jax.ShapeDtypeStruct((rows, d_pool), BF16),
            jax.ShapeDtypeStruct((n_seq, POOL_HIST, d_pool), F32),
            jax.ShapeDtypeStruct((rows, d_model), BF16),
        ],
        scratch_shapes=[
            pltpu.VMEM((nseq, 2 * SUBLANES + tc, d_pool), F32),
            pltpu.VMEM((tile_rows, d_pool), BF16),
        ],
        compiler_params=pltpu.CompilerParams(
            dimension_semantics=("arbitrary",), vmem_limit_bytes=VMEM_LIMIT_BYTES_V7X),
        name="pool_mixer",
    )(x2d, g, w_in_bf, pool_w_bf, pool_scale, cache)


def _conv_kernel(h_ref, wgb_ref, wgc_ref, wxb_ref, wcv_ref, wcg_ref, sw_ref, cw_ref, cb_ref,
                 scache_ref, ccache_ref, yb_ref, zc_ref, news_ref, newc_ref,
                 q_pad, e_pad, s_carry, c_carry, *, nseq, tc, chunks, ks, kc):
    hq = _round_up(ks - 1, SUBLANES)
    he = _round_up(kc - 1, SUBLANES)
    m = pl.program_id(0)
    j = pl.program_id(1)

    if chunks > 1:
        @pl.when((m == 0) & (j == 0))
        def _():
            s_carry[...] = jnp.zeros_like(s_carry)
            c_carry[...] = jnp.zeros_like(c_carry)

    def history(cache_ref, carry_ref, s):
        hist = cache_ref[s]
        if chunks > 1:
            first = (m % chunks == 0).astype(jnp.int32)
            hist = jnp.where(lax.broadcast(first, hist.shape) == 1, hist, carry_ref[j, s])
        return hist

    rc = min(ROW_CHUNK, tc)
    h = h_ref[...]

    glu = _dot(h, wcv_ref[...]) * jax.nn.sigmoid(_dot(h, wcg_ref[...]))
    for s in range(nseq):
        e_pad[s, he - (kc - 1):he, :] = history(ccache_ref, c_carry, s)
        e_pad[s, he:he + tc, :] = glu[s * tc:(s + 1) * tc, :]
        e_pad[s, he + tc:he + tc + SUBLANES, :] = jnp.zeros((SUBLANES, e_pad.shape[-1]), F32)
    def conv31_slab(s, r0):
        base = he - (kc - 1)
        zc = cb_ref[...]
        for a in range(SUBLANES):
            part = None
            for k in range(kc):
                if (base + k) % SUBLANES != a:
                    continue
                t0 = r0 + base + k - a
                term = cw_ref[k:k + 1, :] * e_pad[s, t0:t0 + rc + SUBLANES, :]
                part = term if part is None else part + term
            if part is not None:
                zc = zc + part[a:a + rc, :]
        zc_ref[s * tc + r0:s * tc + r0 + rc, :] = zc

    slabs = [(s, r0) for s in range(nseq) for r0 in range(0, tc, rc)]
    per_gap = -(-len(slabs) // 3)
    for sl in slabs[:per_gap]:
        conv31_slab(*sl)
    gate_c = _dot(h, wgc_ref[...])
    for sl in slabs[per_gap:2 * per_gap]:
        conv31_slab(*sl)
    x_b = _dot(h, wxb_ref[...])
    for sl in slabs[2 * per_gap:]:
        conv31_slab(*sl)
    gate_b = _dot(h, wgb_ref[...])
    for s in range(nseq):
        c_tail = e_pad[s, he + tc - (kc - 1):he + tc, :]
        newc_ref[s] = c_tail
        if chunks > 1:
            c_carry[j, s] = c_tail

    q = gate_c * x_b
    for s in range(nseq):
        q_pad[s, hq - (ks - 1):hq, :] = history(scache_ref, s_carry, s)
        q_pad[s, hq:hq + tc, :] = q[s * tc:(s + 1) * tc, :]
    for s in range(nseq):
        for r0 in range(0, tc, rc):
            rows = slice(s * tc + r0, s * tc + r0 + rc)
            zb = None
            for k in range(ks):
                t0 = hq - (ks - 1) + k + r0
                term = sw_ref[k:k + 1, :] * q_pad[s, t0:t0 + rc, :]
                zb = term if zb is None else zb + term
            yb_ref[rows, :] = (gate_b[rows, :] * zb).astype(yb_ref.dtype)
        s_tail = q_pad[s, hq + tc - (ks - 1):hq + tc, :]
        news_ref[s] = s_tail
        if chunks > 1:
            s_carry[j, s] = s_tail


def _conv_call(h2d, w_conv_blk, l, sconv_w, cconv_w, cconv_b, scache, ccache, *, n_seq, seq_len, tile_rows):
    rows, d_model = h2d.shape
    ks, d_conv = sconv_w.shape[-2:]
    kc = cconv_w.shape[-2]
    cb = CONV_BLOCK
    nj = d_conv // cb
    nseq, tc, chunks = _tile_geometry(n_seq, seq_len, tile_rows)
    n_tiles = rows // tile_rows
    hq = _round_up(ks - 1, SUBLANES)
    he = _round_up(kc - 1, SUBLANES)

    def w_spec(slot):
        off = slot * nj
        return pl.BlockSpec((None, None, d_model, cb), lambda m, j: (l, off + j, 0, 0))

    kern = functools.partial(_conv_kernel, nseq=nseq, tc=tc, chunks=chunks, ks=ks, kc=kc)
    yb, zc, s_tails, c_tails = pl.pallas_call(
        kern,
        grid=(n_tiles, nj),
        in_specs=[
            pl.BlockSpec((tile_rows, d_model), lambda m, j: (m, 0)),
            w_spec(0), w_spec(1), w_spec(2), w_spec(3), w_spec(4),
            pl.BlockSpec((None, ks, cb), lambda m, j: (l, 0, j)),
            pl.BlockSpec((None, kc, cb), lambda m, j: (l, 0, j)),
            pl.BlockSpec((None, 1, cb), lambda m, j: (l, 0, j)),
            pl.BlockSpec((None, nseq, ks - 1, cb), lambda m, j: (l, m // chunks, 0, j)),
            pl.BlockSpec((None, nseq, kc - 1, cb), lambda m, j: (l, m // chunks, 0, j)),
        ],
        out_specs=[
            pl.BlockSpec((tile_rows, cb), lambda m, j: (m, j)),
            pl.BlockSpec((tile_rows, cb), lambda m, j: (m, j)),
            pl.BlockSpec((None, nseq, ks - 1, cb), lambda m, j: (m, 0, 0, j)),
            pl.BlockSpec((None, nseq, kc - 1, cb), lambda m, j: (m, 0, 0, j)),
        ],
        out_shape=[
            jax.ShapeDtypeStruct((rows, d_conv), BF16),
            jax.ShapeDtypeStruct((rows, d_conv), F32),
            jax.ShapeDtypeStruct((n_tiles, nseq, ks - 1, d_conv), F32),
            jax.ShapeDtypeStruct((n_tiles, nseq, kc - 1, d_conv), F32),
        ],
        scratch_shapes=[
            pltpu.VMEM((nseq, hq + tc, cb), F32),
            pltpu.VMEM((nseq, he + tc + SUBLANES, cb), F32),
            pltpu.VMEM((nj, nseq, ks - 1, cb), F32),
            pltpu.VMEM((nj, nseq, kc - 1, cb), F32),
        ],
        compiler_params=pltpu.CompilerParams(
            dimension_semantics=("arbitrary", "arbitrary"), vmem_limit_bytes=VMEM_LIMIT_BYTES_V7X),
        name="conv_mixers",
    )(h2d, w_conv_blk, w_conv_blk, w_conv_blk, w_conv_blk, w_conv_blk, sconv_w, cconv_w, cconv_b, scache, ccache)
    new_sconv = s_tails[chunks - 1::chunks].reshape(n_seq, ks - 1, d_conv)
    new_cconv = c_tails[chunks - 1::chunks].reshape(n_seq, kc - 1, d_conv)
    return yb, zc, new_sconv, new_cconv


def _out_kernel(x_ref, ya_ref, yb_ref, zc_ref, lg_ref, lb_ref, w_ref, o_ref, y_scr, *, d_pool, d_conv):
    @pl.when(pl.program_id(1) == 0)
    def _():
        def body(rows):
            y_scr[rows, :d_pool] = ya_ref[rows, :]
            y_scr[rows, d_pool:d_pool + d_conv] = yb_ref[rows, :]
            z = zc_ref[rows, :]
            zc = z - jnp.mean(z, axis=-1, keepdims=True)
            var = jnp.mean(zc * zc, axis=-1, keepdims=True)
            ln = zc * lax.rsqrt(var + LN_EPS) * lg_ref[...] + lb_ref[...]
            y_scr[rows, d_pool + d_conv:] = (ln * jax.nn.sigmoid(ln)).astype(BF16)
        _for_row_chunks(y_scr.shape[0], body)

    o_ref[...] = x_ref[...] + _dot(y_scr[...], w_ref[...])


def _out_call(x2d, ya, yb, zc, cnorm_g, cnorm_b, w_out_blk, l, *, tile_rows):
    rows, d_model = x2d.shape
    d_pool, d_conv = ya.shape[-1], yb.shape[-1]
    n_col_blocks, d_mix, tile_cols = w_out_blk.shape[1:]
    assert n_col_blocks * tile_cols == d_model
    grid = (rows // tile_rows, n_col_blocks)
    kern = functools.partial(_out_kernel, d_pool=d_pool, d_conv=d_conv)
    return pl.pallas_call(
        kern,
        grid=grid,
        in_specs=[
            pl.BlockSpec((tile_rows, tile_cols), lambda m, n: (m, n)),
            pl.BlockSpec((tile_rows, d_pool), lambda m, n: (m, 0)),
            pl.BlockSpec((tile_rows, d_conv), lambda m, n: (m, 0)),
            pl.BlockSpec((tile_rows, d_conv), lambda m, n: (m, 0)),
            pl.BlockSpec((None, 1, d_conv), lambda m, n: (l, 0, 0)),
            pl.BlockSpec((None, 1, d_conv), lambda m, n: (l, 0, 0)),
            pl.BlockSpec((None, None, d_mix, tile_cols), lambda m, n: (l, n, 0, 0)),
        ],
        out_specs=pl.BlockSpec((tile_rows, tile_cols), lambda m, n: (m, n)),
        out_shape=jax.ShapeDtypeStruct((rows, d_model), F32),
        scratch_shapes=[pltpu.VMEM((tile_rows, d_mix), BF16)],
        compiler_params=pltpu.CompilerParams(
            dimension_semantics=("arbitrary", "arbitrary"), vmem_limit_bytes=VMEM_LIMIT_BYTES_V7X),
        name="out_proj",
    )(x2d, ya, yb, zc, cnorm_g, cnorm_b, w_out_blk)


def _ffn_kernel(x_ref, g_ref, wg0_ref, wu0_ref, wd0_ref, wg1_ref, wu1_ref, wd1_ref, gf_ref, o_ref, h_scr,
                *, final_norm, n_ff_blocks):
    i = pl.program_id(1)
    last = pl.num_programs(1) - 1

    @pl.when(i == 0)
    def _():
        def body(rows):
            h_scr[rows, :] = _rmsnorm_f32(x_ref[rows, :], g_ref[...]).astype(BF16)
            o_ref[rows, :] = jnp.zeros((NORM_ROWS, o_ref.shape[-1]), F32)
        _for_row_chunks(x_ref.shape[0], body)

    def ff_block(wg_ref, wu_ref, wd_ref):
        h = h_scr[...]
        a = _dot(h, wg_ref[...])
        b = _dot(h, wu_ref[...])
        act = (a * jax.nn.sigmoid(a) * b).astype(BF16)
        for c0 in range(0, o_ref.shape[-1], FFN_DOWN_COLS):
            cols = slice(c0, c0 + FFN_DOWN_COLS)
            o_ref[:, cols] += _dot(act, wd_ref[:, cols])

    if n_ff_blocks % 2 == 0:
        ff_block(wg0_ref, wu0_ref, wd0_ref)
        ff_block(wg1_ref, wu1_ref, wd1_ref)
    else:
        @pl.when(i < last)
        def _():
            ff_block(wg0_ref, wu0_ref, wd0_ref)
            ff_block(wg1_ref, wu1_ref, wd1_ref)

        @pl.when(i == last)
        def _():
            ff_block(wg0_ref, wu0_ref, wd0_ref)

    @pl.when(i == last)
    def _():
        def body(rows):
            y = x_ref[rows, :] + o_ref[rows, :]
            if final_norm:
                y = _rmsnorm_f32(y, gf_ref[...])
            o_ref[rows, :] = y
        _for_row_chunks(x_ref.shape[0], body)


def _ffn_call(x2d, g, w_gate_blk, w_up_blk, w_down_bf, l, norm_final, *, tile_rows, final_norm):
    rows, d_model = x2d.shape
    n_ff_blocks, tile_ff = w_gate_blk.shape[1], w_gate_blk.shape[-1]
    assert w_down_bf.shape[-2] == n_ff_blocks * tile_ff
    grid = (rows // tile_rows, -(-n_ff_blocks // 2))
    kern = functools.partial(_ffn_kernel, final_norm=final_norm, n_ff_blocks=n_ff_blocks)

    def w_specs(second):
        f = lambda i: jnp.minimum(2 * i + second, n_ff_blocks - 1)
        return [pl.BlockSpec((None, None, d_model, tile_ff), lambda m, i: (l, f(i), 0, 0)),
                pl.BlockSpec((None, None, d_model, tile_ff), lambda m, i: (l, f(i), 0, 0)),
                pl.BlockSpec((None, tile_ff, d_model), lambda m, i: (l, f(i), 0))]

    return pl.pallas_call(
        kern,
        grid=grid,
        in_specs=[
            pl.BlockSpec((tile_rows, d_model), lambda m, i: (m, 0), pipeline_mode=pl.Buffered(1)),
            pl.BlockSpec((None, 1, d_model), lambda m, i: (l, 0, 0)),
            *w_specs(0), *w_specs(1),
            pl.BlockSpec((1, d_model), lambda m, i: (0, 0)),
        ],
        out_specs=pl.BlockSpec((tile_rows, d_model), lambda m, i: (m, 0)),
        out_shape=jax.ShapeDtypeStruct((rows, d_model), F32),
        scratch_shapes=[pltpu.VMEM((tile_rows, d_model), BF16)],
        compiler_params=pltpu.CompilerParams(
            dimension_semantics=("arbitrary", "arbitrary"), vmem_limit_bytes=VMEM_LIMIT_BYTES_V7X),
        name="swiglu",
    )(x2d, g, w_gate_blk, w_up_blk, w_down_bf, w_gate_blk, w_up_blk, w_down_bf, norm_final)


def _row_tiles(rows):
    small = min(ROW_TILE_SMALL, rows)
    large = min(ROW_TILE_LARGE, rows)
    return dict(pool=small, ffn=small, conv=large, out=large)


def _trunk(x, cache_pool, cache_sconv, cache_cconv, pos0, p):
    n_seq, seq_len, d_model = x.shape
    depth = p["w_pool"].shape[0]
    x2d = x.reshape(n_seq * seq_len, d_model)
    tiles = _row_tiles(x2d.shape[0])
    geo = dict(n_seq=n_seq, seq_len=seq_len)
    pools, sconvs, cconvs = [], [], []
    for l in range(depth):
        ya, new_pool, h2d = _pool_call(x2d, p["norm_mix"], p["w_pool"], l, p["pool_w"], p["pool_scale"], cache_pool,
                                       pos0=pos0, tile_rows=tiles["pool"], **geo)
        yb, zc, new_sconv, new_cconv = _conv_call(h2d, p["w_conv"], l, p["sconv_w"], p["cconv_w"], p["cconv_b"],
                                                  cache_sconv, cache_cconv, tile_rows=tiles["conv"], **geo)
        x2d = _out_call(x2d, ya, yb, zc, p["cnorm_g"], p["cnorm_b"], p["w_out"], l, tile_rows=tiles["out"])
        x2d = _ffn_call(x2d, p["norm_ffn"], p["w_gate"], p["w_up"], p["w_down"], l, p["norm_final"],
                        tile_rows=tiles["ffn"], final_norm=(l == depth - 1))
        pools.append(new_pool)
        sconvs.append(new_sconv)
        cconvs.append(new_cconv)
    return x2d.reshape(x.shape), jnp.stack(pools), jnp.stack(sconvs), jnp.stack(cconvs)


def kernel(x_prompt, x_sample, cache_pool, cache_sconv, cache_cconv, norm_mix, norm_ffn, w_in, pool_w, pool_scale,
           sconv_w, cconv_w, cconv_b, cnorm_g, cnorm_b, w_out, w_gate, w_up, w_down, norm_final):
    depth, d_model = norm_mix.shape
    batch = x_prompt.shape[0]
    dt = x_prompt.dtype
    d_pool = pool_scale.shape[-1]
    row = lambda a: a.reshape(a.shape[0], 1, a.shape[-1])
    p = dict(
        norm_mix=row(norm_mix), norm_ffn=row(norm_ffn), w_pool=w_in[:, :, :d_pool].astype(BF16),
        w_conv=_column_blocks(w_in, CONV_BLOCK, first_col=d_pool), pool_w=pool_w.astype(BF16),
        pool_scale=row(pool_scale), sconv_w=sconv_w, cconv_w=cconv_w, cconv_b=row(cconv_b), cnorm_g=row(cnorm_g),
        cnorm_b=row(cnorm_b), w_out=_column_blocks(w_out, OUT_TILE_COLS), w_gate=_column_blocks(w_gate, FFN_TILE),
        w_up=_column_blocks(w_up, FFN_TILE), w_down=w_down.astype(BF16), norm_final=norm_final.reshape(1, d_model))
    zero_pool = jnp.zeros((depth, batch) + cache_pool.shape[2:], dt)
    zero_sconv = jnp.zeros((depth, batch) + cache_sconv.shape[2:], dt)
    zero_cconv = jnp.zeros((depth, batch) + cache_cconv.shape[2:], dt)
    y_p, pool_p, sconv_p, cconv_p = _trunk(x_prompt, zero_pool, zero_sconv, zero_cconv, 0, p)
    y_s, pool_s, sconv_s, cconv_s = _trunk(x_sample, cache_pool, cache_sconv, cache_cconv, PAST_LEN, p)
    return (y_p, y_s, pool_p, pool_s, sconv_p, sconv_s, cconv_p, cconv_s)
```

```python
import functools

import jax
import jax.numpy as jnp
from jax import lax
from jax.experimental import pallas as pl
from jax.experimental.pallas import tpu as pltpu

POOL_WINDOWS = (2, 4, 8, 16)
POOL_HIST = max(POOL_WINDOWS) - 1
PAST_LEN = 2048
RMS_EPS = 1e-6
LN_EPS = 1e-5

VMEM_LIMIT_BYTES_V7X = 56 * 1024 * 1024
VMEM_LIMIT_BYTES_SWIGLU_V7X = 62 * 1024 * 1024
SUBLANES = 8
NORM_ROWS = 16
NORM_UNROLL = 8
ROW_CHUNK = 64
ROW_TILE_SMALL = 512
ROW_TILE_LARGE = 1024
OUT_TILE_COLS = 512
FFN_TILE = 256
CONV_BLOCK = 256
FFN_DOWN_COLS = 1024

F32 = jnp.float32
BF16 = jnp.bfloat16


def _round_up(n, m):
    return -(-n // m) * m


def _rmsnorm_f32(x, g):
    return x * lax.rsqrt(jnp.mean(x * x, axis=-1, keepdims=True) + RMS_EPS) * g


def _for_row_chunks(n_rows, body):
    def step(i, carry):
        body(pl.ds(pl.multiple_of(i * NORM_ROWS, NORM_ROWS), NORM_ROWS))
        return carry
    lax.fori_loop(0, n_rows // NORM_ROWS, step, 0, unroll=NORM_UNROLL)


def _rmsnorm_rows(x_ref, g_ref, dst_ref):
    def body(rows):
        dst_ref[rows, :] = _rmsnorm_f32(x_ref[rows, :], g_ref[...]).astype(dst_ref.dtype)
    _for_row_chunks(x_ref.shape[0], body)


def _dot(a, b):
    return jnp.dot(a, b, preferred_element_type=F32)


def _cast_block_kernel(w_ref, o_ref):
    o_ref[...] = w_ref[...].astype(o_ref.dtype)


def _column_blocks(w, width, first_col=0):
    depth, k, n = w.shape
    assert first_col % width == 0 and n % width == 0
    first_block = first_col // width
    n_blocks = n // width - first_block
    return pl.pallas_call(
        _cast_block_kernel,
        grid=(depth, n_blocks),
        in_specs=[pl.BlockSpec((None, k, width), lambda d, b: (d, 0, first_block + b))],
        out_specs=pl.BlockSpec((None, None, k, width), lambda d, b: (d, b, 0, 0)),
        out_shape=jax.ShapeDtypeStruct((depth, n_blocks, k, width), BF16),
        compiler_params=pltpu.CompilerParams(
            dimension_semantics=("arbitrary", "arbitrary"), vmem_limit_bytes=VMEM_LIMIT_BYTES_V7X),
        name="cast_column_blocks",
    )(w)


def _tile_geometry(n_seq_total, seq_len, tile_rows):
    if tile_rows >= seq_len:
        assert tile_rows % seq_len == 0
        nseq, tc, chunks = tile_rows // seq_len, seq_len, 1
    else:
        assert seq_len % tile_rows == 0
        nseq, tc, chunks = 1, tile_rows, seq_len // tile_rows
    assert n_seq_total % nseq == 0
    return nseq, tc, chunks


def _pool_kernel(x_ref, g_ref, w_ref, pw_ref, ps_ref, cache_ref, ya_ref, newc_ref, h_ref, p_scr, d_scr,
                 *, nseq, tc, chunks, pos0, group):
    hp = 2 * SUBLANES
    chunk = pl.program_id(0) % chunks
    _rmsnorm_rows(x_ref, g_ref, h_ref)
    v = _dot(h_ref[...], w_ref[...])
    for s in range(nseq):
        @pl.when(chunk == 0)
        def _():
            p_scr[s, hp - POOL_HIST:hp, :] = cache_ref[s]
        p_scr[s, hp:hp + tc, :] = v[s * tc:(s + 1) * tc, :]
    rc = min(ROW_CHUNK, tc)
    for s in range(nseq):
        for r0 in range(0, tc, rc):
            pos = pos0 + chunk * tc + r0 + lax.broadcasted_iota(jnp.int32, (rc, group), 0)
            for gi, k in enumerate(POOL_WINDOWS):
                lanes = slice(gi * group, (gi + 1) * group)
                cur = p_scr[s, hp + r0:hp + r0 + rc, lanes]
                ws = cur
                for i in range(1, k):
                    ws = ws + p_scr[s, hp + r0 - i:hp + r0 - i + rc, lanes]
                cnt = jnp.minimum(k, pos + 1).astype(F32)
                d_scr[s * tc + r0:s * tc + r0 + rc, lanes] = (ws / cnt - cur).astype(BF16)
    for gi in range(len(POOL_WINDOWS)):
        lanes = slice(gi * group, (gi + 1) * group)
        y = _dot(d_scr[:, lanes], pw_ref[gi]) * ps_ref[:, lanes]
        ya_ref[:, lanes] = y.astype(ya_ref.dtype)
    for s in range(nseq):
        tail = p_scr[s, hp + tc - POOL_HIST:hp + tc, :]
        newc_ref[s] = tail
        p_scr[s, hp - POOL_HIST:hp, :] = tail


def _pool_call(x2d, g, w_in_bf, l, pool_w_bf, pool_scale, cache, *, n_seq, seq_len, tile_rows, pos0):
    rows, d_model = x2d.shape
    d_pool = pool_scale.shape[-1]
    ngroups = len(POOL_WINDOWS)
    group = d_pool // ngroups
    nseq, tc, chunks = _tile_geometry(n_seq, seq_len, tile_rows)
    grid = (rows // tile_rows,)
    kern = functools.partial(_pool_kernel, nseq=nseq, tc=tc, chunks=chunks, pos0=pos0, group=group)
    return pl.pallas_call(
        kern,
        grid=grid,
        in_specs=[
            pl.BlockSpec((tile_rows, d_model), lambda m: (m, 0)),
            pl.BlockSpec((None, 1, d_model), lambda m: (l, 0, 0)),
            pl.BlockSpec((None, d_model, d_pool), lambda m: (l, 0, 0)),
            pl.BlockSpec((None, ngroups, group, group), lambda m: (l, 0, 0, 0)),
            pl.BlockSpec((None, 1, d_pool), lambda m: (l, 0, 0)),
            pl.BlockSpec((None, nseq, POOL_HIST, d_pool), lambda m: (l, m // chunks, 0, 0)),
        ],
        out_specs=[
            pl.BlockSpec((tile_rows, d_pool), lambda m: (m, 0)),
            pl.BlockSpec((nseq, POOL_HIST, d_pool), lambda m: (m // chunks, 0, 0)),
            pl.BlockSpec((tile_rows, d_model), lambda m: (m, 0)),
        ],
        out_shape=[
            jax.ShapeDtypeStruct((rows, d_pool), BF16),
            jax.ShapeDtypeStruct((n_seq, POOL_HIST, d_pool), F32),
            jax.ShapeDtypeStruct((rows, d_model), BF16),
        ],
        scratch_shapes=[
            pltpu.VMEM((nseq, 2 * SUBLANES + tc, d_pool), F32),
            pltpu.VMEM((tile_rows, d_pool), BF16),
        ],
        compiler_params=pltpu.CompilerParams(
            dimension_semantics=("arbitrary",), vmem_limit_bytes=VMEM_LIMIT_BYTES_V7X),
        name="pool_mixer",
    )(x2d, g, w_in_bf, pool_w_bf, pool_scale, cache)


def _conv_kernel(h_ref, wgb_ref, wgc_ref, wxb_ref, wcv_ref, wcg_ref, sw_ref, cw_ref, cb_ref,
                 scache_ref, ccache_ref, yb_ref, zc_ref, news_ref, newc_ref,
                 q_pad, e_pad, s_carry, c_carry, *, nseq, tc, chunks, ks, kc):
    hq = _round_up(ks - 1, SUBLANES)
    he = _round_up(kc - 1, SUBLANES)
    m = pl.program_id(0)
    j = pl.program_id(1)

    if chunks > 1:
        @pl.when((m == 0) & (j == 0))
        def _():
            s_carry[...] = jnp.zeros_like(s_carry)
            c_carry[...] = jnp.zeros_like(c_carry)

    def history(cache_ref, carry_ref, s):
        hist = cache_ref[s]
        if chunks > 1:
            first = (m % chunks == 0).astype(jnp.int32)
            hist = jnp.where(lax.broadcast(first, hist.shape) == 1, hist, carry_ref[j, s])
        return hist

    rc = min(ROW_CHUNK, tc)
    h = h_ref[...]

    glu = _dot(h, wcv_ref[...]) * jax.nn.sigmoid(_dot(h, wcg_ref[...]))
    for s in range(nseq):
        e_pad[s, he - (kc - 1):he, :] = history(ccache_ref, c_carry, s)
        e_pad[s, he:he + tc, :] = glu[s * tc:(s + 1) * tc, :]
        e_pad[s, he + tc:he + tc + SUBLANES, :] = jnp.zeros((SUBLANES, e_pad.shape[-1]), F32)
    def conv31_slab(s, r0):
        base = he - (kc - 1)
        zc = cb_ref[...]
        for a in range(SUBLANES):
            part = None
            for k in range(kc):
                if (base + k) % SUBLANES != a:
                    continue
                t0 = r0 + base + k - a
                term = cw_ref[k:k + 1, :] * e_pad[s, t0:t0 + rc + SUBLANES, :]
                part = term if part is None else part + term
            if part is not None:
                zc = zc + part[a:a + rc, :]
        zc_ref[s * tc + r0:s * tc + r0 + rc, :] = zc

    slabs = [(s, r0) for s in range(nseq) for r0 in range(0, tc, rc)]
    per_gap = -(-len(slabs) // 3)
    for sl in slabs[:per_gap]:
        conv31_slab(*sl)
    gate_c = _dot(h, wgc_ref[...])
    for sl in slabs[per_gap:2 * per_gap]:
        conv31_slab(*sl)
    x_b = _dot(h, wxb_ref[...])
    for sl in slabs[2 * per_gap:]:
        conv31_slab(*sl)
    gate_b = _dot(h, wgb_ref[...])
    for s in range(nseq):
        c_tail = e_pad[s, he + tc - (kc - 1):he + tc, :]
        newc_ref[s] = c_tail
        if chunks > 1:
            c_carry[j, s] = c_tail

    q = gate_c * x_b
    for s in range(nseq):
        q_pad[s, hq - (ks - 1):hq, :] = history(scache_ref, s_carry, s)
        q_pad[s, hq:hq + tc, :] = q[s * tc:(s + 1) * tc, :]
    for s in range(nseq):
        for r0 in range(0, tc, rc):
            rows = slice(s * tc + r0, s * tc + r0 + rc)
            zb = None
            for k in range(ks):
                t0 = hq - (ks - 1) + k + r0
                term = sw_ref[k:k + 1, :] * q_pad[s, t0:t0 + rc, :]
                zb = term if zb is None else zb + term
            yb_ref[rows, :] = (gate_b[rows, :] * zb).astype(yb_ref.dtype)
        s_tail = q_pad[s, hq + tc - (ks - 1):hq + tc, :]
        news_ref[s] = s_tail
        if chunks > 1:
            s_carry[j, s] = s_tail


def _conv_call(h2d, w_conv_blk, l, sconv_w, cconv_w, cconv_b, scache, ccache, *, n_seq, seq_len, tile_rows):
    rows, d_model = h2d.shape
    ks, d_conv = sconv_w.shape[-2:]
    kc = cconv_w.shape[-2]
    cb = CONV_BLOCK
    nj = d_conv // cb
    nseq, tc, chunks = _tile_geometry(n_seq, seq_len, tile_rows)
    n_tiles = rows // tile_rows
    hq = _round_up(ks - 1, SUBLANES)
    he = _round_up(kc - 1, SUBLANES)

    def w_spec(slot):
        off = slot * nj
        return pl.BlockSpec((None, None, d_model, cb), lambda m, j: (l, off + j, 0, 0))

    kern = functools.partial(_conv_kernel, nseq=nseq, tc=tc, chunks=chunks, ks=ks, kc=kc)
    yb, zc, s_tails, c_tails = pl.pallas_call(
        kern,
        grid=(n_tiles, nj),
        in_specs=[
            pl.BlockSpec((tile_rows, d_model), lambda m, j: (m, 0)),
            w_spec(0), w_spec(1), w_spec(2), w_spec(3), w_spec(4),
            pl.BlockSpec((None, ks, cb), lambda m, j: (l, 0, j)),
            pl.BlockSpec((None, kc, cb), lambda m, j: (l, 0, j)),
            pl.BlockSpec((None, 1, cb), lambda m, j: (l, 0, j)),
            pl.BlockSpec((None, nseq, ks - 1, cb), lambda m, j: (l, m // chunks, 0, j)),
            pl.BlockSpec((None, nseq, kc - 1, cb), lambda m, j: (l, m // chunks, 0, j)),
        ],
        out_specs=[
            pl.BlockSpec((tile_rows, cb), lambda m, j: (m, j)),
            pl.BlockSpec((tile_rows, cb), lambda m, j: (m, j)),
            pl.BlockSpec((None, nseq, ks - 1, cb), lambda m, j: (m, 0, 0, j)),
            pl.BlockSpec((None, nseq, kc - 1, cb), lambda m, j: (m, 0, 0, j)),
        ],
        out_shape=[
            jax.ShapeDtypeStruct((rows, d_conv), BF16),
            jax.ShapeDtypeStruct((rows, d_conv), F32),
            jax.ShapeDtypeStruct((n_tiles, nseq, ks - 1, d_conv), F32),
            jax.ShapeDtypeStruct((n_tiles, nseq, kc - 1, d_conv), F32),
        ],
        scratch_shapes=[
            pltpu.VMEM((nseq, hq + tc, cb), F32),
            pltpu.VMEM((nseq, he + tc + SUBLANES, cb), F32),
            pltpu.VMEM((nj, nseq, ks - 1, cb), F32),
            pltpu.VMEM((nj, nseq, kc - 1, cb), F32),
        ],
        compiler_params=pltpu.CompilerParams(
            dimension_semantics=("arbitrary", "arbitrary"), vmem_limit_bytes=VMEM_LIMIT_BYTES_V7X),
        name="conv_mixers",
    )(h2d, w_conv_blk, w_conv_blk, w_conv_blk, w_conv_blk, w_conv_blk, sconv_w, cconv_w, cconv_b, scache, ccache)
    new_sconv = s_tails[chunks - 1::chunks].reshape(n_seq, ks - 1, d_conv)
    new_cconv = c_tails[chunks - 1::chunks].reshape(n_seq, kc - 1, d_conv)
    return yb, zc, new_sconv, new_cconv


def _out_kernel(x_ref, ya_ref, yb_ref, zc_ref, lg_ref, lb_ref, w_ref, o_ref, y_scr, *, d_pool, d_conv):
    @pl.when(pl.program_id(1) == 0)
    def _():
        def body(rows):
            y_scr[rows, :d_pool] = ya_ref[rows, :]
            y_scr[rows, d_pool:d_pool + d_conv] = yb_ref[rows, :]
            z = zc_ref[rows, :]
            zc = z - jnp.mean(z, axis=-1, keepdims=True)
            var = jnp.mean(zc * zc, axis=-1, keepdims=True)
            ln = zc * lax.rsqrt(var + LN_EPS) * lg_ref[...] + lb_ref[...]
            y_scr[rows, d_pool + d_conv:] = (ln * jax.nn.sigmoid(ln)).astype(BF16)
        _for_row_chunks(y_scr.shape[0], body)

    o_ref[...] = x_ref[...] + _dot(y_scr[...], w_ref[...])


def _out_call(x2d, ya, yb, zc, cnorm_g, cnorm_b, w_out_blk, l, *, tile_rows):
    rows, d_model = x2d.shape
    d_pool, d_conv = ya.shape[-1], yb.shape[-1]
    n_col_blocks, d_mix, tile_cols = w_out_blk.shape[1:]
    assert n_col_blocks * tile_cols == d_model
    grid = (rows // tile_rows, n_col_blocks)
    kern = functools.partial(_out_kernel, d_pool=d_pool, d_conv=d_conv)
    return pl.pallas_call(
        kern,
        grid=grid,
        in_specs=[
            pl.BlockSpec((tile_rows, tile_cols), lambda m, n: (m, n)),
            pl.BlockSpec((tile_rows, d_pool), lambda m, n: (m, 0)),
            pl.BlockSpec((tile_rows, d_conv), lambda m, n: (m, 0)),
            pl.BlockSpec((tile_rows, d_conv), lambda m, n: (m, 0)),
            pl.BlockSpec((None, 1, d_conv), lambda m, n: (l, 0, 0)),
            pl.BlockSpec((None, 1, d_conv), lambda m, n: (l, 0, 0)),
            pl.BlockSpec((None, None, d_mix, tile_cols), lambda m, n: (l, n, 0, 0)),
        ],
        out_specs=pl.BlockSpec((tile_rows, tile_cols), lambda m, n: (m, n)),
        out_shape=jax.ShapeDtypeStruct((rows, d_model), F32),
        scratch_shapes=[pltpu.VMEM((tile_rows, d_mix), BF16)],
        compiler_params=pltpu.CompilerParams(
            dimension_semantics=("arbitrary", "arbitrary"), vmem_limit_bytes=VMEM_LIMIT_BYTES_V7X),
        name="out_proj",
    )(x2d, ya, yb, zc, cnorm_g, cnorm_b, w_out_blk)


def _ffn_kernel(x_ref, g_ref, wg0_ref, wu0_ref, wd0_ref, wg1_ref, wu1_ref, wd1_ref, gf_ref, o_ref, h_scr,
                *, final_norm, n_ff_blocks):
    i = pl.program_id(1)
    last = pl.num_programs(1) - 1

    @pl.when(i == 0)
    def _():
        def body(rows):
            h_scr[rows, :] = _rmsnorm_f32(x_ref[rows, :], g_ref[...]).astype(BF16)
            o_ref[rows, :] = jnp.zeros((NORM_ROWS, o_ref.shape[-1]), F32)
        _for_row_chunks(x_ref.shape[0], body)

    def ff_block(wg_ref, wu_ref, wd_ref):
        h = h_scr[...]
        a = _dot(h, wg_ref[...])
        b = _dot(h, wu_ref[...])
        act = (a * jax.nn.sigmoid(a) * b).astype(BF16)
        for c0 in range(0, o_ref.shape[-1], FFN_DOWN_COLS):
            cols = slice(c0, c0 + FFN_DOWN_COLS)
            o_ref[:, cols] += _dot(act, wd_ref[:, cols])

    if n_ff_blocks % 2 == 0:
        ff_block(wg0_ref, wu0_ref, wd0_ref)
        ff_block(wg1_ref, wu1_ref, wd1_ref)
    else:
        @pl.when(i < last)
        def _():
            ff_block(wg0_ref, wu0_ref, wd0_ref)
            ff_block(wg1_ref, wu1_ref, wd1_ref)

        @pl.when(i == last)
        def _():
            ff_block(wg0_ref, wu0_ref, wd0_ref)

    @pl.when(i == last)
    def _():
        def body(rows):
            y = x_ref[rows, :] + o_ref[rows, :]
            if final_norm:
                y = _rmsnorm_f32(y, gf_ref[...])
            o_ref[rows, :] = y
        _for_row_chunks(x_ref.shape[0], body)


def _ffn_call(x2d, g, w_gate_blk, w_up_blk, w_down_bf, l, norm_final, *, tile_rows, final_norm):
    rows, d_model = x2d.shape
    n_ff_blocks, tile_ff = w_gate_blk.shape[1], w_gate_blk.shape[-1]
    assert w_down_bf.shape[-2] == n_ff_blocks * tile_ff
    grid = (rows // tile_rows, -(-n_ff_blocks // 2))
    kern = functools.partial(_ffn_kernel, final_norm=final_norm, n_ff_blocks=n_ff_blocks)

    def w_specs(second):
        f = lambda i: jnp.minimum(2 * i + second, n_ff_blocks - 1)
        return [pl.BlockSpec((None, None, d_model, tile_ff), lambda m, i: (l, f(i), 0, 0)),
                pl.BlockSpec((None, None, d_model, tile_ff), lambda m, i: (l, f(i), 0, 0)),
                pl.BlockSpec((None, tile_ff, d_model), lambda m, i: (l, f(i), 0))]

    return pl.pallas_call(
        kern,
        grid=grid,
        in_specs=[
            pl.BlockSpec((tile_rows, d_model), lambda m, i: (m, 0)),
            pl.BlockSpec((None, 1, d_model), lambda m, i: (l, 0, 0)),
            *w_specs(0), *w_specs(1),
            pl.BlockSpec((1, d_model), lambda m, i: (0, 0)),
        ],
        out_specs=pl.BlockSpec((tile_rows, d_model), lambda m, i: (m, 0)),
        out_shape=jax.ShapeDtypeStruct((rows, d_model), F32),
        scratch_shapes=[pltpu.VMEM((tile_rows, d_model), BF16)],
        compiler_params=pltpu.CompilerParams(
            dimension_semantics=("arbitrary", "arbitrary"), vmem_limit_bytes=VMEM_LIMIT_BYTES_SWIGLU_V7X),
        name="swiglu",
    )(x2d, g, w_gate_blk, w_up_blk, w_down_bf, w_gate_blk, w_up_blk, w_down_bf, norm_final)


def _row_tiles(rows):
    small = min(ROW_TILE_SMALL, rows)
    large = min(ROW_TILE_LARGE, rows)
    return dict(pool=small, ffn=small, conv=large, out=large)


def _trunk(x, cache_pool, cache_sconv, cache_cconv, pos0, p):
    n_seq, seq_len, d_model = x.shape
    depth = p["w_pool"].shape[0]
    x2d = x.reshape(n_seq * seq_len, d_model)
    tiles = _row_tiles(x2d.shape[0])
    geo = dict(n_seq=n_seq, seq_len=seq_len)
    pools, sconvs, cconvs = [], [], []
    for l in range(depth):
        ya, new_pool, h2d = _pool_call(x2d, p["norm_mix"], p["w_pool"], l, p["pool_w"], p["pool_scale"], cache_pool,
                                       pos0=pos0, tile_rows=tiles["pool"], **geo)
        yb, zc, new_sconv, new_cconv = _conv_call(h2d, p["w_conv"], l, p["sconv_w"], p["cconv_w"], p["cconv_b"],
                                                  cache_sconv, cache_cconv, tile_rows=tiles["conv"], **geo)
        x2d = _out_call(x2d, ya, yb, zc, p["cnorm_g"], p["cnorm_b"], p["w_out"], l, tile_rows=tiles["out"])
        x2d = _ffn_call(x2d, p["norm_ffn"], p["w_gate"], p["w_up"], p["w_down"], l, p["norm_final"],
                        tile_rows=tiles["ffn"], final_norm=(l == depth - 1))
        pools.append(new_pool)
        sconvs.append(new_sconv)
        cconvs.append(new_cconv)
    return x2d.reshape(x.shape), jnp.stack(pools), jnp.stack(sconvs), jnp.stack(cconvs)


def kernel(x_prompt, x_sample, cache_pool, cache_sconv, cache_cconv, norm_mix, norm_ffn, w_in, pool_w, pool_scale,
           sconv_w, cconv_w, cconv_b, cnorm_g, cnorm_b, w_out, w_gate, w_up, w_down, norm_final):
    depth, d_model = norm_mix.shape
    batch = x_prompt.shape[0]
    dt = x_prompt.dtype
    d_pool = pool_scale.shape[-1]
    row = lambda a: a.reshape(a.shape[0], 1, a.shape[-1])
    p = dict(
        norm_mix=row(norm_mix), norm_ffn=row(norm_ffn), w_pool=w_in[:, :, :d_pool].astype(BF16),
        w_conv=_column_blocks(w_in, CONV_BLOCK, first_col=d_pool), pool_w=pool_w.astype(BF16),
        pool_scale=row(pool_scale), sconv_w=sconv_w, cconv_w=cconv_w, cconv_b=row(cconv_b), cnorm_g=row(cnorm_g),
        cnorm_b=row(cnorm_b), w_out=_column_blocks(w_out, OUT_TILE_COLS), w_gate=_column_blocks(w_gate, FFN_TILE),
        w_up=_column_blocks(w_up, FFN_TILE), w_down=w_down.astype(BF16), norm_final=norm_final.reshape(1, d_model))
    zero_pool = jnp.zeros((depth, batch) + cache_pool.shape[2:], dt)
    zero_sconv = jnp.zeros((depth, batch) + cache_sconv.shape[2:], dt)
    zero_cconv = jnp.zeros((depth, batch) + cache_cconv.shape[2:], dt)
    y_p, pool_p, sconv_p, cconv_p = _trunk(x_prompt, zero_pool, zero_sconv, zero_cconv, 0, p)
    y_s, pool_s, sconv_s, cconv_s = _trunk(x_sample, cache_pool, cache_sconv, cache_cconv, PAST_LEN, p)
    return (y_p, y_s, pool_p, pool_s, sconv_p, sconv_s, cconv_p, cconv_s)
```
